```python
import math
import jax, jax.numpy as jnp
from jax import lax
import numpy as np

D_MODEL = 2048
BATCH = 1
SEQ = 8192
DEPTH = 1
DEC_BATCH = 32
DEC_SEQ = 8
PAST_LEN = 8192
PAGE_SIZE = 128

N_META = 16
D_MIX = D_MODEL
D_ATTN = D_MIX // 2
D_POOL = D_MIX - D_ATTN
N_HEADS = 8
HEAD_V = D_ATTN // N_HEADS
HEAD_QK = HEAD_V // 2
POOL_WINDOWS = (2, 4, 8, 16)
N_POOL_GROUPS = len(POOL_WINDOWS)
POOL_GROUP = D_POOL // N_POOL_GROUPS
POOL_HIST = max(POOL_WINDOWS) - 1
D_FF = 4 * D_MODEL
D_IN = 3 * D_ATTN + D_POOL
Q_BLOCK = 128
EPS = 1e-6

kernel_name = "hymba_diffattn_pool_decoder_step"


def rmsnorm(x, g):
    xf = x.astype(jnp.float32)
    y = xf * lax.rsqrt(jnp.mean(xf * xf, axis=-1, keepdims=True) + EPS)
    return (y * g.astype(jnp.float32)).astype(x.dtype)


def alibi_slopes():
    return 2.0 ** (-8.0 * jnp.arange(1, N_HEADS + 1, dtype=jnp.float32) / N_HEADS)


def in_proj(xn, w_in):
    z = xn @ w_in
    b, l = z.shape[:2]
    q = z[..., :D_ATTN].reshape(b, l, N_HEADS, 2 * HEAD_QK)
    k = z[..., D_ATTN:2 * D_ATTN].reshape(b, l, N_HEADS, 2 * HEAD_QK)
    v = z[..., 2 * D_ATTN:3 * D_ATTN].reshape(b, l, N_HEADS, HEAD_V)
    u = z[..., 3 * D_ATTN:]
    return q, k, v, u


def diff_lambda(lq1, lk1, lq2, lk2, lam_init):
    f = jnp.float32
    return (jnp.exp(jnp.sum(lq1.astype(f) * lk1.astype(f)))
            - jnp.exp(jnp.sum(lq2.astype(f) * lk2.astype(f))) + lam_init)


def diff_attn(q, k, v, q_pos, k_pos, lam, subln_g, lam_init):
    b, lq = q.shape[:2]
    lk = k.shape[1]
    qf = q.astype(jnp.float32).reshape(b, lq, N_HEADS, 2, HEAD_QK)
    kf = k.astype(jnp.float32).reshape(b, lk, N_HEADS, 2, HEAD_QK)
    s = jnp.einsum('bqhmd,bkhmd->bmhqk', qf, kf) * (HEAD_QK ** -0.5)
    dist = (q_pos[:, None] - k_pos[None, :]).astype(jnp.float32)
    bias = -alibi_slopes()[:, None, None] * dist
    visible = k_pos[None, :] <= q_pos[:, None]
    s = jnp.where(visible[None, None, None], s + bias[None, None], -jnp.inf)
    p = jax.nn.softmax(s, axis=-1)
    a = p[:, 0] - lam * p[:, 1]
    o = jnp.einsum('bhqk,bkhd->bqhd', a, v.astype(jnp.float32))
    o = rmsnorm(o, subln_g) * (1.0 - lam_init)
    return o.astype(q.dtype)


def prompt_attention(q, k, v, lam, subln_g, lam_init):
    b, t = q.shape[:2]
    pad = (-t) % Q_BLOCK
    padw = ((0, 0), (0, pad), (0, 0), (0, 0))
    qp, kp, vp = jnp.pad(q, padw), jnp.pad(k, padw), jnp.pad(v, padw)
    tp = t + pad
    nblk = tp // Q_BLOCK
    k_pos = jnp.arange(tp, dtype=jnp.int32)
    q_blocks = qp.reshape(b, nblk, Q_BLOCK, N_HEADS, 2 * HEAD_QK).transpose(1, 0, 2, 3, 4)

    def one_block(args):
        qb, i = args
        q_pos = i * Q_BLOCK + jnp.arange(Q_BLOCK, dtype=jnp.int32)
        return diff_attn(qb, kp, vp, q_pos, k_pos, lam, subln_g, lam_init)

    o = lax.map(one_block, (q_blocks, jnp.arange(nblk, dtype=jnp.int32)))
    o = o.transpose(1, 0, 2, 3, 4).reshape(b, tp, N_HEADS, HEAD_V)[:, :t]
    return o.reshape(b, t, D_ATTN)


def pool_mix(u_ext, n_hist, w_pool, pool_scale):
    b, le = u_ext.shape[:2]
    l = le - n_hist
    uf = u_ext.astype(jnp.float32)
    c = jnp.concatenate([jnp.zeros_like(uf[:, :1]), jnp.cumsum(uf, axis=1)], axis=1)
    end = np.arange(n_hist, le) + 1
    outs = []
    for g, w in enumerate(POOL_WINDOWS):
        start = np.maximum(end - w, 0)
        cnt = (end - start).astype(np.float32)
        cg = c[:, :, g * POOL_GROUP:(g + 1) * POOL_GROUP]
        mean = (jnp.take(cg, end, axis=1) - jnp.take(cg, start, axis=1)) / cnt[None, :, None]
        outs.append(mean - uf[:, n_hist:, g * POOL_GROUP:(g + 1) * POOL_GROUP])
    d = jnp.stack(outs, axis=2)
    y = jnp.einsum('blgc,gce->blge', d, w_pool.astype(jnp.float32)).reshape(b, l, D_POOL)
    return (y * pool_scale.astype(jnp.float32)).astype(u_ext.dtype)


def channel_mlp(h, g, w_up, w_down):
    a = jax.nn.relu(rmsnorm(h, g) @ w_up)
    return h + (a * a) @ w_down


def setup_inputs(seed: int = 0) -> dict:
    key = jax.random.key(seed)
    ks = jax.random.split(key, 24)
    f = jnp.float32
    n_pages = PAST_LEN // PAGE_SIZE
    n_used = DEC_BATCH * n_pages
    n_phys = n_used + n_used // 4

    def nrm(k, shape, scale):
        return jax.random.normal(k, shape, f) * scale

    page_table = jax.random.permutation(ks[5], n_phys)[:n_used].reshape(DEC_BATCH, n_pages).astype(jnp.int32)
    return {
        "x_prompt": nrm(ks[0], (BATCH, SEQ, D_MODEL), 1.0),
        "x_sample": nrm(ks[1], (DEC_BATCH, DEC_SEQ, D_MODEL), 1.0),
        "cache_k": nrm(ks[2], (DEPTH, n_phys, PAGE_SIZE, N_HEADS, 2 * HEAD_QK), 1.0),
        "cache_v": nrm(ks[3], (DEPTH, n_phys, PAGE_SIZE, N_HEADS, HEAD_V), 1.0),
        "state_pool": nrm(ks[4], (DEPTH, DEC_BATCH, POOL_HIST, D_POOL), 1.0),
        "page_table": page_table,
        "meta_tokens": nrm(ks[6], (N_META, D_MODEL), 1.0),
        "norm_mix_g": 1.0 + nrm(ks[7], (DEPTH, D_MODEL), 0.02),
        "w_in": nrm(ks[8], (DEPTH, D_MODEL, D_IN), D_MODEL ** -0.5),
        "lambda_q1": nrm(ks[9], (DEPTH, HEAD_QK), 0.1),
        "lambda_k1": nrm(ks[10], (DEPTH, HEAD_QK), 0.1),
        "lambda_q2": nrm(ks[11], (DEPTH, HEAD_QK), 0.1),
        "lambda_k2": nrm(ks[12], (DEPTH, HEAD_QK), 0.1),
        "subln_g": 1.0 + nrm(ks[13], (DEPTH, HEAD_V), 0.02),
        "w_pool": nrm(ks[14], (DEPTH, N_POOL_GROUPS, POOL_GROUP, POOL_GROUP), POOL_GROUP ** -0.5),
        "pool_scale": 1.0 + nrm(ks[15], (DEPTH, D_POOL), 0.1),
        "w_out": nrm(ks[16], (DEPTH, D_MIX, D_MODEL), D_MIX ** -0.5),
        "norm_mlp_g": 1.0 + nrm(ks[17], (DEPTH, D_MODEL), 0.02),
        "w_up": nrm(ks[18], (DEPTH, D_MODEL, D_FF), D_MODEL ** -0.5),
        "w_down": nrm(ks[19], (DEPTH, D_FF, D_MODEL), D_FF ** -0.5),
        "norm_final_g": 1.0 + nrm(ks[20], (D_MODEL,), 0.02),
    }


def reference(x_prompt, x_sample, cache_k, cache_v, state_pool, page_table,
              meta_tokens, norm_mix_g, w_in, lambda_q1, lambda_k1, lambda_q2, lambda_k2,
              subln_g, w_pool, pool_scale, w_out, norm_mlp_g, w_up, w_down, norm_final_g):
    b = x_prompt.shape[0]
    bd, ld = x_sample.shape[:2]
    past = page_table.shape[1] * PAGE_SIZE
    hp = jnp.concatenate(
        [jnp.broadcast_to(meta_tokens[None], (b, N_META, D_MODEL)).astype(x_prompt.dtype), x_prompt], axis=1)
    t = hp.shape[1]
    hs = x_sample
    pos_s = past + jnp.arange(ld, dtype=jnp.int32)
    k_pos_s = jnp.concatenate([jnp.arange(past, dtype=jnp.int32), pos_s])

    kp_l, vp_l, pp_l, ks_l, vs_l, ps_l = [], [], [], [], [], []
    for l in range(DEPTH):
        lam_init = 0.8 - 0.6 * math.exp(-0.3 * l)
        lam = diff_lambda(lambda_q1[l], lambda_k1[l], lambda_q2[l], lambda_k2[l], lam_init)

        q, k, v, u = in_proj(rmsnorm(hp, norm_mix_g[l]), w_in[l])
        a_p = prompt_attention(q, k, v, lam, subln_g[l], lam_init)
        m_p = pool_mix(u, 0, w_pool[l], pool_scale[l])
        hp = hp + jnp.concatenate([a_p, m_p], axis=-1) @ w_out[l]
        hp = channel_mlp(hp, norm_mlp_g[l], w_up[l], w_down[l])
        kp_l.append(k)
        vp_l.append(v)
        pp_l.append(u[:, -POOL_HIST:])

        qs, kn, vn, us = in_proj(rmsnorm(hs, norm_mix_g[l]), w_in[l])
        k_past = cache_k[l][page_table].reshape(bd, past, N_HEADS, 2 * HEAD_QK)
        v_past = cache_v[l][page_table].reshape(bd, past, N_HEADS, HEAD_V)
        k_all = jnp.concatenate([k_past.astype(kn.dtype), kn], axis=1)
        v_all = jnp.concatenate([v_past.astype(vn.dtype), vn], axis=1)
        a_s = diff_attn(qs, k_all, v_all, pos_s, k_pos_s, lam, subln_g[l], lam_init).reshape(bd, ld, D_ATTN)
        u_ext = jnp.concatenate([state_pool[l].astype(us.dtype), us], axis=1)
        m_s = pool_mix(u_ext, POOL_HIST, w_pool[l], pool_scale[l])
        hs = hs + jnp.concatenate([a_s, m_s], axis=-1) @ w_out[l]
        hs = channel_mlp(hs, norm_mlp_g[l], w_up[l], w_down[l])
        ks_l.append(kn)
        vs_l.append(vn)
        ps_l.append(u_ext[:, -POOL_HIST:])

    y_prompt = rmsnorm(hp[:, N_META:], norm_final_g)
    y_sample = rmsnorm(hs, norm_final_g)
    return (y_prompt, y_sample, jnp.stack(kp_l), jnp.stack(vp_l), jnp.stack(pp_l),
            jnp.stack(ks_l), jnp.stack(vs_l), jnp.stack(ps_l))
```

```python
import functools
import math

import jax
import jax.numpy as jnp
from jax import lax
from jax.experimental import pallas as pl
from jax.experimental.pallas import tpu as pltpu

N_HEADS = 8
HEAD_V = 128
HEAD_QK = HEAD_V // 2
D_ATTN = N_HEADS * HEAD_V
POOL_WINDOWS = (2, 4, 8, 16)
POOL_HALO = 16
EPS = 1e-6
MASK_VALUE = -1e30
QK_SCALE = HEAD_QK ** -0.5

VMEM_LIMIT_BYTES = 56 * 1024 * 1024
BF16 = jnp.bfloat16
F32 = jnp.float32

_NT = (((1,), (1,)), ((), ()))


def _params(n_grid_dims):
    return pltpu.CompilerParams(dimension_semantics=("arbitrary",) * n_grid_dims,
                                vmem_limit_bytes=VMEM_LIMIT_BYTES)


def _row_tile(rows, target):
    best = rows
    for t in range(16, min(rows, target) + 1, 16):
        if rows % t == 0:
            best = t
    return best if best <= target else rows


def _rmsnorm(x, g):
    ms = jnp.mean(x * x, axis=-1, keepdims=True)
    return x * lax.rsqrt(ms + EPS) * g


def _in_proj_kernel(x_ref, g_ref, w_ref, z_ref, xn_ref):
    @pl.when(pl.program_id(1) == 0)
    def _():
        xn_ref[...] = _rmsnorm(x_ref[...], g_ref[...]).astype(BF16)

    z_ref[...] = jnp.dot(xn_ref[...], w_ref[...], preferred_element_type=F32)


def _in_proj(x, g, w, tm_target=512, tn=1024):
    rows, d = x.shape
    n = w.shape[1]
    tm = _row_tile(rows, tm_target)
    return pl.pallas_call(
        _in_proj_kernel,
        grid=(rows // tm, n // tn),
        in_specs=[pl.BlockSpec((tm, d), lambda i, j: (i, 0)),
                  pl.BlockSpec((1, d), lambda i, j: (0, 0)),
                  pl.BlockSpec((d, tn), lambda i, j: (0, j))],
        out_specs=pl.BlockSpec((tm, tn), lambda i, j: (i, j)),
        out_shape=jax.ShapeDtypeStruct((rows, n), F32),
        scratch_shapes=[pltpu.VMEM((tm, d), BF16)],
        compiler_params=_params(2),
        name="in_proj",
    )(x, g, w)


def _stack_maps(q):
    q = q * QK_SCALE
    lane = lax.broadcasted_iota(jnp.int32, q.shape, 1)
    q1 = jnp.where(lane < HEAD_QK, q, 0.0)
    q2 = jnp.where(lane >= HEAD_QK, q, 0.0)
    return jnp.concatenate([q1, q2], axis=0).astype(BF16)


def _diff_lambda(lq1_ref, lk1_ref, lq2_ref, lk2_ref, lam_init):
    a = jnp.sum(lq1_ref[...] * lk1_ref[...], axis=-1, keepdims=True)
    b = jnp.sum(lq2_ref[...] * lk2_ref[...], axis=-1, keepdims=True)
    return jnp.exp(a) - jnp.exp(b) + lam_init


def _diff_combine(acc, l, lam, g, lam_init):
    n = acc.shape[0] // 2
    o = acc[:n] / l[:n] - lam * (acc[n:] / l[n:])
    return _rmsnorm(o, g) * (1.0 - lam_init)


def _prompt_attn_kernel(slope_ref, q_ref, k_ref, v_ref, km_ref, vm_ref,
                        lq1_ref, lk1_ref, lq2_ref, lk2_ref, g_ref, o_ref,
                        q2_ref, m_ref, l_ref, acc_ref, bias_ref, dbias_ref,
                        *, tq, n_meta, lam_init):
    h = pl.program_id(0)
    i = pl.program_id(1)
    slope = slope_ref[h]

    @pl.when(i == 0)
    def _():
        row = lax.broadcasted_iota(jnp.int32, (2 * tq, tq), 0)
        col = lax.broadcasted_iota(jnp.int32, (2 * tq, tq), 1)
        row = jnp.where(row >= tq, row - tq, row)
        b = slope * (col - row).astype(F32)
        bias_ref[...] = b
        dbias_ref[...] = jnp.where(col <= row, b, MASK_VALUE)

    q2_ref[...] = _stack_maps(q_ref[...])

    row = lax.broadcasted_iota(jnp.int32, (2 * tq, n_meta), 0)
    col = lax.broadcasted_iota(jnp.int32, (2 * tq, n_meta), 1)
    row = jnp.where(row >= tq, row - tq, row)
    qpos = n_meta + i * tq + row
    s = lax.dot_general(q2_ref[...], km_ref[...].astype(BF16), _NT, preferred_element_type=F32)
    s = s + slope * (col - qpos).astype(F32)
    m0 = jnp.max(s, axis=-1, keepdims=True)
    p = jnp.exp(s - m0)
    m_ref[...] = m0
    l_ref[...] = jnp.sum(p, axis=-1, keepdims=True)
    acc_ref[...] = jnp.dot(p.astype(BF16), vm_ref[...].astype(BF16), preferred_element_type=F32)

    def chunk(c, bias):
        start = pl.multiple_of(c * tq, tq)
        kc = k_ref[pl.ds(start, tq), :].astype(BF16)
        vc = v_ref[pl.ds(start, tq), :].astype(BF16)
        s = lax.dot_general(q2_ref[...], kc, _NT, preferred_element_type=F32) + bias
        sigma = slope * (tq * (c - i)).astype(F32)
        m_old = m_ref[...]
        m_new = jnp.maximum(m_old, jnp.max(s, axis=-1, keepdims=True) + sigma)
        p = jnp.exp(s - (m_new - sigma))
        alpha = jnp.exp(m_old - m_new)
        l_ref[...] = alpha * l_ref[...] + jnp.sum(p, axis=-1, keepdims=True)
        acc_ref[...] = alpha * acc_ref[...] + jnp.dot(p.astype(BF16), vc,
                                                      preferred_element_type=F32)
        m_ref[...] = m_new

    def body(c, carry):
        chunk(c, bias_ref[...])
        return carry

    lax.fori_loop(0, i, body, 0)
    chunk(i, dbias_ref[...])

    lam = _diff_lambda(lq1_ref, lk1_ref, lq2_ref, lk2_ref, lam_init)
    o_ref[...] = _diff_combine(acc_ref[...], l_ref[...], lam, g_ref[...], lam_init).astype(o_ref.dtype)


def _prompt_attention(z_b, z_s, meta_row0, slopes, lam_vecs, subln_g, lam_init, n_meta, tq=512):
    seq = z_b.shape[0]
    tq = _row_tile(seq, tq)
    assert tq % 128 == 0 and meta_row0 % n_meta == 0
    meta_blk = meta_row0 // n_meta
    vec = pl.BlockSpec((1, HEAD_QK), lambda h, i: (0, 0))
    kern = functools.partial(_prompt_attn_kernel, tq=tq, n_meta=n_meta, lam_init=lam_init)
    return pl.pallas_call(
        kern,
        grid=(N_HEADS, seq // tq),
        in_specs=[pl.BlockSpec(memory_space=pltpu.SMEM),
                  pl.BlockSpec((tq, HEAD_V), lambda h, i: (i, h)),
                  pl.BlockSpec((seq, HEAD_V), lambda h, i: (0, N_HEADS + h)),
                  pl.BlockSpec((seq, HEAD_V), lambda h, i: (0, 2 * N_HEADS + h)),
                  pl.BlockSpec((n_meta, HEAD_V), lambda h, i: (meta_blk, N_HEADS + h)),
                  pl.BlockSpec((n_meta, HEAD_V), lambda h, i: (meta_blk, 2 * N_HEADS + h)),
                  vec, vec, vec, vec,
                  pl.BlockSpec((1, HEAD_V), lambda h, i: (0, 0))],
        out_specs=pl.BlockSpec((tq, HEAD_V), lambda h, i: (i, h)),
        out_shape=jax.ShapeDtypeStruct((seq, D_ATTN), BF16),
        scratch_shapes=[pltpu.VMEM((2 * tq, HEAD_V), BF16),
                        pltpu.VMEM((2 * tq, 1), F32),
                        pltpu.VMEM((2 * tq, 1), F32),
                        pltpu.VMEM((2 * tq, HEAD_V), F32),
                        pltpu.VMEM((2 * tq, tq), F32),
                        pltpu.VMEM((2 * tq, tq), F32)],
        compiler_params=_params(2),
        name="prompt_attn",
    )(slopes, z_b, z_b, z_b, z_s, z_s, *lam_vecs, subln_g)


def _meta_attn_kernel(slope_ref, q_ref, k_ref, v_ref, lq1_ref, lk1_ref, lq2_ref, lk2_ref, g_ref,
                      o_ref, *, n_meta, lam_init):
    slope = slope_ref[pl.program_id(0)]
    q2 = _stack_maps(q_ref[...])
    row = lax.broadcasted_iota(jnp.int32, (2 * n_meta, n_meta), 0)
    col = lax.broadcasted_iota(jnp.int32, (2 * n_meta, n_meta), 1)
    row = jnp.where(row >= n_meta, row - n_meta, row)
    s = lax.dot_general(q2, k_ref[...].astype(BF16), _NT, preferred_element_type=F32)
    s = jnp.where(col <= row, s + slope * (col - row).astype(F32), MASK_VALUE)
    p = jnp.exp(s - jnp.max(s, axis=-1, keepdims=True))
    l = jnp.sum(p, axis=-1, keepdims=True)
    acc = jnp.dot(p.astype(BF16), v_ref[...].astype(BF16), preferred_element_type=F32)
    lam = _diff_lambda(lq1_ref, lk1_ref, lq2_ref, lk2_ref, lam_init)
    o_ref[...] = _diff_combine(acc, l, lam, g_ref[...], lam_init)


def _meta_attention(z_s, meta_row0, slopes, lam_vecs, subln_g, lam_init, n_meta):
    meta_blk = meta_row0 // n_meta
    vec = pl.BlockSpec((1, HEAD_QK), lambda h: (0, 0))
    kern = functools.partial(_meta_attn_kernel, n_meta=n_meta, lam_init=lam_init)
    return pl.pallas_call(
        kern,
        grid=(N_HEADS,),
        in_specs=[pl.BlockSpec(memory_space=pltpu.SMEM),
                  pl.BlockSpec((n_meta, HEAD_V), lambda h: (meta_blk, h)),
                  pl.BlockSpec((n_meta, HEAD_V), lambda h: (meta_blk, N_HEADS + h)),
                  pl.BlockSpec((n_meta, HEAD_V), lambda h: (meta_blk, 2 * N_HEADS + h)),
                  vec, vec, vec, vec,
                  pl.BlockSpec((1, HEAD_V), lambda h: (0, 0))],
        out_specs=pl.BlockSpec((n_meta, HEAD_V), lambda h: (0, h)),
        out_shape=jax.ShapeDtypeStruct((n_meta, D_ATTN), F32),
        compiler_params=_params(1),
        name="meta_attn",
    )(slopes, z_s, z_s, z_s, *lam_vecs, subln_g)


def _sample_attn_kernel(pt_ref, slope_ref, q_ref, kn_ref, vn_ref, *rest,
                        pages_per_step, page_size, ld, past, lam_init):
    del pt_ref
    P = pages_per_step
    k_pages = rest[:P]
    v_pages = rest[P:2 * P]
    lq1_ref, lk1_ref, lq2_ref, lk2_ref, g_ref, o_ref, q2_ref, m_ref, l_ref, acc_ref = rest[2 * P:]
    c = pl.program_id(1)
    tk = P * page_size
    rows = 2 * ld

    @pl.when(c == 0)
    def _():
        for h in range(N_HEADS):
            q2_ref[h] = _stack_maps(q_ref[:, h * HEAD_V:(h + 1) * HEAD_V])
        m_ref[...] = jnp.full(m_ref.shape, MASK_VALUE, F32)
        l_ref[...] = jnp.zeros(l_ref.shape, F32)
        acc_ref[...] = jnp.zeros(acc_ref.shape, F32)

    row = lax.broadcasted_iota(jnp.int32, (rows, tk), 0)
    col = lax.broadcasted_iota(jnp.int32, (rows, tk), 1)
    qi = jnp.where(row >= ld, row - ld, row)
    rel = (c * tk + col - (past + qi)).astype(F32)

    def update(h, s, v):
        m_old = m_ref[h]
        m_new = jnp.maximum(m_old, jnp.max(s, axis=-1, keepdims=True))
        p = jnp.exp(s - m_new)
        alpha = jnp.exp(m_old - m_new)
        l_ref[h] = alpha * l_ref[h] + jnp.sum(p, axis=-1, keepdims=True)
        acc_ref[h] = alpha * acc_ref[h] + jnp.dot(p.astype(BF16), v, preferred_element_type=F32)
        m_ref[h] = m_new

    for h in range(N_HEADS):
        kh = jnp.concatenate([kp[0, pl.ds(h, page_size, stride=N_HEADS), :] for kp in k_pages], axis=0)
        vh = jnp.concatenate([vp[0, pl.ds(h, page_size, stride=N_HEADS), :] for vp in v_pages], axis=0)
        s = lax.dot_general(q2_ref[h], kh.astype(BF16), _NT, preferred_element_type=F32)
        update(h, s + slope_ref[h] * rel, vh.astype(BF16))

    @pl.when(c == pl.num_programs(1) - 1)
    def _():
        lam = _diff_lambda(lq1_ref, lk1_ref, lq2_ref, lk2_ref, lam_init)
        row = lax.broadcasted_iota(jnp.int32, (rows, rows), 0)
        col = lax.broadcasted_iota(jnp.int32, (rows, rows), 1)
        qi = jnp.where(row >= ld, row - ld, row)
        visible = col <= qi
        reln = (col - qi).astype(F32)
        pad = jnp.zeros((ld, HEAD_V), F32)
        for h in range(N_HEADS):
            kn = jnp.concatenate([kn_ref[:, h * HEAD_V:(h + 1) * HEAD_V], pad], axis=0).astype(BF16)
            vn = jnp.concatenate([vn_ref[:, h * HEAD_V:(h + 1) * HEAD_V], pad], axis=0).astype(BF16)
            s = lax.dot_general(q2_ref[h], kn, _NT, preferred_element_type=F32)
            s = jnp.where(visible, s + slope_ref[h] * reln, MASK_VALUE)
            update(h, s, vn)
            o_ref[:, h * HEAD_V:(h + 1) * HEAD_V] = _diff_combine(acc_ref[h], l_ref[h], lam,
                                                                  g_ref[...], lam_init)


def _sample_attention(z_s, cache_k, cache_v, page_table, slopes, lam_vecs, subln_g, lam_init, ld):
    bd, n_pages = page_table.shape
    n_phys, page_size = cache_k.shape[0], cache_k.shape[1]
    past = n_pages * page_size
    P = max(p for p in range(1, 9) if n_pages % p == 0)
    k2 = cache_k.reshape(n_phys, page_size * N_HEADS, HEAD_V)
    v2 = cache_v.reshape(n_phys, page_size * N_HEADS, HEAD_V)
    vec = pl.BlockSpec((1, HEAD_QK), lambda b, c, pt: (0, 0))

    def page_spec(j):
        return pl.BlockSpec((1, page_size * N_HEADS, HEAD_V),
                            lambda b, c, pt: (pt[b * n_pages + c * P + j], 0, 0))

    kern = functools.partial(_sample_attn_kernel, pages_per_step=P, page_size=page_size, ld=ld,
                             past=past, lam_init=lam_init)
    grid_spec = pltpu.PrefetchScalarGridSpec(
        num_scalar_prefetch=1,
        grid=(bd, n_pages // P),
        in_specs=[pl.BlockSpec(memory_space=pltpu.SMEM),
                  pl.BlockSpec((ld, D_ATTN), lambda b, c, pt: (b, 0)),
                  pl.BlockSpec((ld, D_ATTN), lambda b, c, pt: (b, 1)),
                  pl.BlockSpec((ld, D_ATTN), lambda b, c, pt: (b, 2))]
                 + [page_spec(j) for j in range(P)] + [page_spec(j) for j in range(P)]
                 + [vec, vec, vec, vec, pl.BlockSpec((1, HEAD_V), lambda b, c, pt: (0, 0))],
        out_specs=pl.BlockSpec((ld, D_ATTN), lambda b, c, pt: (b, 0)),
        scratch_shapes=[pltpu.VMEM((N_HEADS, 2 * ld, HEAD_V), BF16),
                        pltpu.VMEM((N_HEADS, 2 * ld, 1), F32),
                        pltpu.VMEM((N_HEADS, 2 * ld, 1), F32),
                        pltpu.VMEM((N_HEADS, 2 * ld, HEAD_V), F32)])
    return pl.pallas_call(
        kern,
        grid_spec=grid_spec,
        out_shape=jax.ShapeDtypeStruct((bd * ld, D_ATTN), F32),
        compiler_params=_params(2),
        name="sample_attn",
    )(page_table.reshape(-1), slopes, z_s, z_s, z_s, *([k2] * P), *([v2] * P), *lam_vecs, subln_g)


def _window_sums(ext):
    out = {}
    s = ext
    w = 1
    while w < max(POOL_WINDOWS):
        s = s + pltpu.roll(s, w, 0)
        w *= 2
        out[w] = s
    return out


def _pool_project(u, sums, inv_cnt, wp_ref, scale_ref):
    cg = u.shape[1] // len(POOL_WINDOWS)
    outs = []
    for g, w in enumerate(POOL_WINDOWS):
        sl = slice(g * cg, (g + 1) * cg)
        d = sums[w][:, sl] * inv_cnt[w] - u[:, sl]
        outs.append(jnp.dot(d.astype(BF16), wp_ref[g], preferred_element_type=F32))
    return jnp.concatenate(outs, axis=1) * scale_ref[...]


def _pool_big_kernel(u_ref, prev_ref, meta_ref, wp_ref, scale_ref, m_ref):
    halo = jnp.where(pl.program_id(0) == 0, meta_ref[...], prev_ref[...])
    u = u_ref[...]
    sums = _window_sums(jnp.concatenate([halo, u], axis=0))
    sums = {w: s[POOL_HALO:] for w, s in sums.items()}
    inv = {w: 1.0 / w for w in POOL_WINDOWS}
    m_ref[...] = _pool_project(u, sums, inv, wp_ref, scale_ref).astype(m_ref.dtype)


def _pool_big(z_b, z_s, meta_row0, w_pool, pool_scale, tm=512):
    seq = z_b.shape[0]
    d_pool = w_pool.shape[0] * w_pool.shape[1]
    tm = _row_tile(seq, tm)
    ucol = z_b.shape[1] // d_pool - 1
    per = tm // POOL_HALO
    return pl.pallas_call(
        _pool_big_kernel,
        grid=(seq // tm,),
        in_specs=[pl.BlockSpec((tm, d_pool), lambda i: (i, ucol)),
                  pl.BlockSpec((POOL_HALO, d_pool), lambda i: (jnp.maximum(i * per - 1, 0), ucol)),
                  pl.BlockSpec((POOL_HALO, d_pool), lambda i: (meta_row0 // POOL_HALO, ucol)),
                  pl.BlockSpec(w_pool.shape, lambda i: (0, 0, 0)),
                  pl.BlockSpec((1, d_pool), lambda i: (0, 0))],
        out_specs=pl.BlockSpec((tm, d_pool), lambda i: (i, 0)),
        out_shape=jax.ShapeDtypeStruct((seq, d_pool), BF16),
        compiler_params=_params(1),
        name="pool_big",
    )(z_b, z_b, z_s, w_pool, pool_scale)


def _pool_small_kernel(ext_ref, wp_ref, scale_ref, m_ref, *, bd, ld, n_meta):
    grp = POOL_HALO + ld
    ext = ext_ref[...]
    sums = _window_sums(ext)
    meta0 = bd * grp + POOL_HALO

    def new_rows(a):
        parts = [a[b * grp + POOL_HALO:(b + 1) * grp] for b in range(bd)]
        return jnp.concatenate(parts + [a[meta0:meta0 + n_meta]], axis=0)

    n = bd * ld + n_meta
    r = lax.broadcasted_iota(jnp.int32, (n, 1), 0)
    inv = {}
    for w in POOL_WINDOWS:
        cnt = jnp.where(r < bd * ld, w, jnp.minimum(w, r - bd * ld + 1))
        inv[w] = 1.0 / cnt.astype(F32)
    sums = {w: new_rows(s) for w, s in sums.items()}
    m_ref[...] = _pool_project(new_rows(ext), sums, inv, wp_ref, scale_ref).astype(m_ref.dtype)


def _pool_small(ext, w_pool, pool_scale, bd, ld, n_meta):
    n = bd * ld + n_meta
    d_pool = ext.shape[1]
    kern = functools.partial(_pool_small_kernel, bd=bd, ld=ld, n_meta=n_meta)
    return pl.pallas_call(
        kern,
        grid=(1,),
        in_specs=[pl.BlockSpec(ext.shape, lambda i: (0, 0)),
                  pl.BlockSpec(w_pool.shape, lambda i: (0, 0, 0)),
                  pl.BlockSpec((1, d_pool), lambda i: (0, 0))],
        out_specs=pl.BlockSpec((n, d_pool), lambda i: (0, 0)),
        out_shape=jax.ShapeDtypeStruct((n, d_pool), BF16),
        compiler_params=_params(1),
        name="pool_small",
    )(ext, w_pool, pool_scale)


def _out_proj_kernel(x_ref, a_ref, m_ref, wa_ref, wm_ref, h_ref):
    h_ref[...] = (x_ref[...]
                  + jnp.dot(a_ref[...].astype(BF16), wa_ref[...], preferred_element_type=F32)
                  + jnp.dot(m_ref[...], wm_ref[...], preferred_element_type=F32))


def _out_proj(x, a, m, w_out, tm=512):
    rows, d = x.shape
    da, dm = a.shape[1], m.shape[1]
    tm = _row_tile(rows, tm)
    return pl.pallas_call(
        _out_proj_kernel,
        grid=(rows // tm,),
        in_specs=[pl.BlockSpec((tm, d), lambda i: (i, 0)),
                  pl.BlockSpec((tm, da), lambda i: (i, 0)),
                  pl.BlockSpec((tm, dm), lambda i: (i, 0)),
                  pl.BlockSpec((da, d), lambda i: (0, 0)),
                  pl.BlockSpec((dm, d), lambda i: (da // dm, 0))],
        out_specs=pl.BlockSpec((tm, d), lambda i: (i, 0)),
        out_shape=jax.ShapeDtypeStruct((rows, d), F32),
        compiler_params=_params(1),
        name="out_proj",
    )(x, a, m, w_out, w_out)


def _mlp_kernel(h_ref, g_ref, wu_ref, wd_ref, gf_ref, y_ref, xn_ref):
    j = pl.program_id(1)

    @pl.when(j == 0)
    def _():
        h = h_ref[...]
        xn_ref[...] = _rmsnorm(h, g_ref[...]).astype(BF16)
        y_ref[...] = h

    a = jnp.maximum(jnp.dot(xn_ref[...], wu_ref[...], preferred_element_type=F32), 0.0)
    y_ref[...] += jnp.dot((a * a).astype(BF16), wd_ref[...], preferred_element_type=F32)

    @pl.when(j == pl.num_programs(1) - 1)
    def _():
        y_ref[...] = _rmsnorm(y_ref[...], gf_ref[...])


def _mlp(h, g, w_up, w_down, g_final, tm=512, tf=1024):
    rows, d = h.shape
    d_ff = w_up.shape[1]
    tm = _row_tile(rows, tm)
    return pl.pallas_call(
        _mlp_kernel,
        grid=(rows // tm, d_ff // tf),
        in_specs=[pl.BlockSpec((tm, d), lambda i, j: (i, 0)),
                  pl.BlockSpec((1, d), lambda i, j: (0, 0)),
                  pl.BlockSpec((d, tf), lambda i, j: (0, j)),
                  pl.BlockSpec((tf, d), lambda i, j: (j, 0)),
                  pl.BlockSpec((1, d), lambda i, j: (0, 0))],
        out_specs=pl.BlockSpec((tm, d), lambda i, j: (i, 0)),
        out_shape=jax.ShapeDtypeStruct((rows, d), F32),
        scratch_shapes=[pltpu.VMEM((tm, d), BF16)],
        compiler_params=_params(2),
        name="mlp",
    )(h, g, w_up, w_down, g_final)


def kernel(x_prompt, x_sample, cache_k, cache_v, state_pool, page_table, meta_tokens, norm_mix_g,
           w_in, lambda_q1, lambda_k1, lambda_q2, lambda_k2, subln_g, w_pool, pool_scale, w_out,
           norm_mlp_g, w_up, w_down, norm_final_g):
    depth = norm_mix_g.shape[0]
    batch, seq, d_model = x_prompt.shape
    bd, ld, _ = x_sample.shape
    n_meta = meta_tokens.shape[0]
    assert depth == 1 and batch == 1, "one layer and one prompt sequence are supported"
    assert n_meta == POOL_HALO and (bd * ld) % n_meta == 0 and seq >= POOL_HALO
    layer = 0
    lam_init = 0.8 - 0.6 * math.exp(-0.3 * layer)
    slopes = 2.0 ** (-8.0 * jnp.arange(1, N_HEADS + 1, dtype=F32) / N_HEADS)
    n_small = bd * ld
    d_pool = w_pool.shape[1] * w_pool.shape[2]

    x_big = x_prompt.reshape(seq, d_model)
    x_small = jnp.concatenate([x_sample.reshape(n_small, d_model), meta_tokens.astype(F32)], axis=0)

    g_mix = norm_mix_g[layer].reshape(1, d_model)
    g_mlp = norm_mlp_g[layer].reshape(1, d_model)
    g_final = norm_final_g.reshape(1, d_model)
    g_sub = subln_g[layer].reshape(1, HEAD_V)
    scale_pool = pool_scale[layer].reshape(1, d_pool)
    lam_vecs = [v[layer].reshape(1, HEAD_QK) for v in (lambda_q1, lambda_k1, lambda_q2, lambda_k2)]
    w_in_b = w_in[layer].astype(BF16)
    w_out_b = w_out[layer].astype(BF16)
    w_up_b = w_up[layer].astype(BF16)
    w_down_b = w_down[layer].astype(BF16)
    w_pool_b = w_pool[layer].astype(BF16)

    z_b = _in_proj(x_big, g_mix, w_in_b)
    z_s = _in_proj(x_small, g_mix, w_in_b)

    a_b = _prompt_attention(z_b, z_s, n_small, slopes, lam_vecs, g_sub, lam_init, n_meta)
    a_meta = _meta_attention(z_s, n_small, slopes, lam_vecs, g_sub, lam_init, n_meta)
    a_samp = _sample_attention(z_s, cache_k[layer], cache_v[layer], page_table, slopes, lam_vecs,
                               g_sub, lam_init, ld)
    a_s = jnp.concatenate([a_samp, a_meta], axis=0)

    u_b = z_b[:, 3 * D_ATTN:]
    u_s = z_s[:n_small, 3 * D_ATTN:].reshape(bd, ld, d_pool)
    u_meta = z_s[n_small:, 3 * D_ATTN:]
    hist = state_pool[layer].astype(F32)
    n_hist = hist.shape[1]
    zpad = jnp.zeros((bd, POOL_HALO - n_hist, d_pool), F32)
    ext = jnp.concatenate([jnp.concatenate([zpad, hist, u_s], axis=1).reshape(-1, d_pool),
                           jnp.zeros((POOL_HALO, d_pool), F32), u_meta], axis=0)
    m_b = _pool_big(z_b, z_s, n_small, w_pool_b, scale_pool)
    m_s = _pool_small(ext, w_pool_b, scale_pool, bd, ld, n_meta)

    h_b = _out_proj(x_big, a_b, m_b, w_out_b)
    h_s = _out_proj(x_small, a_s, m_s, w_out_b)
    y_b = _mlp(h_b, g_mlp, w_up_b, w_down_b, g_final)
    y_s = _mlp(h_s, g_mlp, w_up_b, w_down_b, g_final)

    def heads(z, col):
        return z[:, col * D_ATTN:(col + 1) * D_ATTN].reshape(-1, N_HEADS, HEAD_V)

    t = seq + n_meta
    k_prompt = jnp.concatenate([heads(z_s[n_small:], 1), heads(z_b, 1)], axis=0)
    v_prompt = jnp.concatenate([heads(z_s[n_small:], 2), heads(z_b, 2)], axis=0)
    u_ext = jnp.concatenate([hist, u_s], axis=1)
    return (y_b.reshape(1, seq, d_model),
            y_s[:n_small].reshape(bd, ld, d_model),
            k_prompt.reshape(1, 1, t, N_HEADS, HEAD_V),
            v_prompt.reshape(1, 1, t, N_HEADS, HEAD_V),
            u_b[-n_hist:].reshape(1, 1, n_hist, d_pool),
            heads(z_s[:n_small], 1).reshape(1, bd, ld, N_HEADS, HEAD_V),
            heads(z_s[:n_small], 2).reshape(1, bd, ld, N_HEADS, HEAD_V),
            u_ext[:, -n_hist:].reshape(1, bd, n_hist, d_pool))
```

```python
import functools
import math

import jax
import jax.numpy as jnp
from jax import lax
from jax.experimental import pallas as pl
from jax.experimental.pallas import tpu as pltpu

N_HEADS = 8
HEAD_V = 128
HEAD_QK = HEAD_V // 2
D_ATTN = N_HEADS * HEAD_V
POOL_WINDOWS = (2, 4, 8, 16)
POOL_HALO = 16
EPS = 1e-6
MASK_VALUE = -1e30
QK_SCALE = HEAD_QK ** -0.5

VMEM_LIMIT_BYTES = 56 * 1024 * 1024
BF16 = jnp.bfloat16
F32 = jnp.float32

_NT = (((1,), (1,)), ((), ()))


def _params(n_grid_dims):
    return pltpu.CompilerParams(dimension_semantics=("arbitrary",) * n_grid_dims,
                                vmem_limit_bytes=VMEM_LIMIT_BYTES)


def _row_tile(rows, target):
    best = rows
    for t in range(16, min(rows, target) + 1, 16):
        if rows % t == 0:
            best = t
    return best if best <= target else rows


def _rmsnorm(x, g):
    ms = jnp.mean(x * x, axis=-1, keepdims=True)
    return x * lax.rsqrt(ms + EPS) * g


def _in_proj_kernel(x_ref, g_ref, w_ref, z_ref, xn_ref):
    @pl.when(pl.program_id(1) == 0)
    def _():
        xn_ref[...] = _rmsnorm(x_ref[...], g_ref[...]).astype(BF16)

    z_ref[...] = jnp.dot(xn_ref[...], w_ref[...], preferred_element_type=F32)


def _in_proj(x, g, w, tm_target=512, tn=2048):
    rows, d = x.shape
    n = w.shape[1]
    tm = _row_tile(rows, tm_target)
    return pl.pallas_call(
        _in_proj_kernel,
        grid=(rows // tm, n // tn),
        in_specs=[pl.BlockSpec((tm, d), lambda i, j: (i, 0)),
                  pl.BlockSpec((1, d), lambda i, j: (0, 0)),
                  pl.BlockSpec((d, tn), lambda i, j: (0, j))],
        out_specs=pl.BlockSpec((tm, tn), lambda i, j: (i, j)),
        out_shape=jax.ShapeDtypeStruct((rows, n), F32),
        scratch_shapes=[pltpu.VMEM((tm, d), BF16)],
        compiler_params=_params(2),
        name="in_proj",
    )(x, g, w)


def _stack_maps(q):
    q = q * QK_SCALE
    lane = lax.broadcasted_iota(jnp.int32, q.shape, 1)
    q1 = jnp.where(lane < HEAD_QK, q, 0.0)
    q2 = jnp.where(lane >= HEAD_QK, q, 0.0)
    return jnp.concatenate([q1, q2], axis=0).astype(BF16)


def _diff_lambda(lq1_ref, lk1_ref, lq2_ref, lk2_ref, lam_init):
    a = jnp.sum(lq1_ref[...] * lk1_ref[...], axis=-1, keepdims=True)
    b = jnp.sum(lq2_ref[...] * lk2_ref[...], axis=-1, keepdims=True)
    return jnp.exp(a) - jnp.exp(b) + lam_init


def _diff_combine(acc, l, lam, g, lam_init):
    n = acc.shape[0] // 2
    o = acc[:n] / l[:n] - lam * (acc[n:] / l[n:])
    return _rmsnorm(o, g) * (1.0 - lam_init)


def _prompt_attn_kernel(slope_ref, q_ref, k_ref, v_ref, km_ref, vm_ref,
                        lq1_ref, lk1_ref, lq2_ref, lk2_ref, gcol_ref, o_ref,
                        kb_ref, vt_ref, kmb_ref, vmt_ref, q2_ref, m_ref, l_ref, acc_ref,
                        bias_ref, dbias_ref, *, tq, n_meta, lam_init):
    h = pl.program_id(0)
    i = pl.program_id(1)
    slope = slope_ref[h]
    n_chunks = kb_ref.shape[0]

    @pl.when(i == 0)
    def _():
        for c in range(n_chunks):
            kb_ref[c] = k_ref[c * tq:(c + 1) * tq, :].astype(BF16)
            vt_ref[c] = v_ref[c * tq:(c + 1) * tq, :].T.astype(BF16)
        pad = jnp.zeros((HEAD_V - n_meta, HEAD_V), F32)
        kmb_ref[...] = jnp.concatenate([km_ref[...], pad], axis=0).astype(BF16)
        vmt_ref[...] = jnp.concatenate([vm_ref[...], pad], axis=0).T.astype(BF16)
        key = lax.broadcasted_iota(jnp.int32, (tq, 2 * tq), 0)
        qry = lax.broadcasted_iota(jnp.int32, (tq, 2 * tq), 1)
        qry = jnp.where(qry >= tq, qry - tq, qry)
        b = slope * (key - qry).astype(F32)
        bias_ref[...] = b
        dbias_ref[...] = jnp.where(key <= qry, b, MASK_VALUE)

    q2_ref[...] = _stack_maps(q_ref[...])

    key = lax.broadcasted_iota(jnp.int32, (HEAD_V, 2 * tq), 0)
    qry = lax.broadcasted_iota(jnp.int32, (HEAD_V, 2 * tq), 1)
    qry = jnp.where(qry >= tq, qry - tq, qry)
    qpos = n_meta + i * tq + qry
    s = lax.dot_general(kmb_ref[...], q2_ref[...], _NT, preferred_element_type=F32)
    s = jnp.where(key < n_meta, s + slope * (key - qpos).astype(F32), MASK_VALUE)
    m0 = jnp.max(s, axis=0, keepdims=True)
    p = jnp.exp(s - m0)
    m_ref[...] = m0
    l_ref[...] = jnp.sum(p, axis=0, keepdims=True)
    acc_ref[...] = jnp.dot(vmt_ref[...], p.astype(BF16), preferred_element_type=F32)

    def chunk(c, bias):
        s = lax.dot_general(kb_ref[c], q2_ref[...], _NT, preferred_element_type=F32) + bias
        sigma = slope * (tq * (c - i)).astype(F32)
        m_old = m_ref[...]
        m_new = jnp.maximum(m_old, jnp.max(s, axis=0, keepdims=True) + sigma)
        p = jnp.exp(s - (m_new - sigma))
        alpha = jnp.exp(m_old - m_new)
        l_ref[...] = alpha * l_ref[...] + jnp.sum(p, axis=0, keepdims=True)
        acc_ref[...] = alpha * acc_ref[...] + jnp.dot(vt_ref[c], p.astype(BF16),
                                                      preferred_element_type=F32)
        m_ref[...] = m_new

    def body(c, carry):
        chunk(c, bias_ref[...])
        return carry

    lax.fori_loop(0, i, body, 0)
    chunk(i, dbias_ref[...])

    lam = _diff_lambda(lq1_ref, lk1_ref, lq2_ref, lk2_ref, lam_init)
    acc = acc_ref[...]
    l = l_ref[...]
    o = acc[:, :tq] / l[:, :tq] - lam * (acc[:, tq:] / l[:, tq:])
    ms = jnp.mean(o * o, axis=0, keepdims=True)
    o = o * lax.rsqrt(ms + EPS) * gcol_ref[...] * (1.0 - lam_init)
    o_ref[...] = o.T.astype(o_ref.dtype)


def _prompt_attention(z_b, z_s, meta_row0, slopes, lam_vecs, subln_g, lam_init, n_meta, tq=512):
    seq = z_b.shape[0]
    tq = _row_tile(seq, tq)
    assert tq % 128 == 0 and meta_row0 % n_meta == 0 and n_meta <= HEAD_V
    meta_blk = meta_row0 // n_meta
    vec = pl.BlockSpec((1, HEAD_QK), lambda h, i: (0, 0))
    kern = functools.partial(_prompt_attn_kernel, tq=tq, n_meta=n_meta, lam_init=lam_init)
    return pl.pallas_call(
        kern,
        grid=(N_HEADS, seq // tq),
        in_specs=[pl.BlockSpec(memory_space=pltpu.SMEM),
                  pl.BlockSpec((tq, HEAD_V), lambda h, i: (i, h)),
                  pl.BlockSpec((seq, HEAD_V), lambda h, i: (0, N_HEADS + h)),
                  pl.BlockSpec((seq, HEAD_V), lambda h, i: (0, 2 * N_HEADS + h)),
                  pl.BlockSpec((n_meta, HEAD_V), lambda h, i: (meta_blk, N_HEADS + h)),
                  pl.BlockSpec((n_meta, HEAD_V), lambda h, i: (meta_blk, 2 * N_HEADS + h)),
                  vec, vec, vec, vec,
                  pl.BlockSpec((HEAD_V, 1), lambda h, i: (0, 0))],
        out_specs=pl.BlockSpec((tq, HEAD_V), lambda h, i: (i, h)),
        out_shape=jax.ShapeDtypeStruct((seq, D_ATTN), BF16),
        scratch_shapes=[pltpu.VMEM((seq // tq, tq, HEAD_V), BF16),
                        pltpu.VMEM((seq // tq, HEAD_V, tq), BF16),
                        pltpu.VMEM((HEAD_V, HEAD_V), BF16),
                        pltpu.VMEM((HEAD_V, HEAD_V), BF16),
                        pltpu.VMEM((2 * tq, HEAD_V), BF16),
                        pltpu.VMEM((1, 2 * tq), F32),
                        pltpu.VMEM((1, 2 * tq), F32),
                        pltpu.VMEM((HEAD_V, 2 * tq), F32),
                        pltpu.VMEM((tq, 2 * tq), F32),
                        pltpu.VMEM((tq, 2 * tq), F32)],
        compiler_params=_params(2),
        name="prompt_attn",
    )(slopes, z_b, z_b, z_b, z_s, z_s, *lam_vecs, subln_g.reshape(HEAD_V, 1))


def _meta_attn_kernel(slope_ref, q_ref, k_ref, v_ref, lq1_ref, lk1_ref, lq2_ref, lk2_ref, g_ref,
                      o_ref, *, n_meta, lam_init):
    slope = slope_ref[pl.program_id(0)]
    q2 = _stack_maps(q_ref[...])
    row = lax.broadcasted_iota(jnp.int32, (2 * n_meta, n_meta), 0)
    col = lax.broadcasted_iota(jnp.int32, (2 * n_meta, n_meta), 1)
    row = jnp.where(row >= n_meta, row - n_meta, row)
    s = lax.dot_general(q2, k_ref[...].astype(BF16), _NT, preferred_element_type=F32)
    s = jnp.where(col <= row, s + slope * (col - row).astype(F32), MASK_VALUE)
    p = jnp.exp(s - jnp.max(s, axis=-1, keepdims=True))
    l = jnp.sum(p, axis=-1, keepdims=True)
    acc = jnp.dot(p.astype(BF16), v_ref[...].astype(BF16), preferred_element_type=F32)
    lam = _diff_lambda(lq1_ref, lk1_ref, lq2_ref, lk2_ref, lam_init)
    o_ref[...] = _diff_combine(acc, l, lam, g_ref[...], lam_init)


def _meta_attention(z_s, meta_row0, slopes, lam_vecs, subln_g, lam_init, n_meta):
    meta_blk = meta_row0 // n_meta
    vec = pl.BlockSpec((1, HEAD_QK), lambda h: (0, 0))
    kern = functools.partial(_meta_attn_kernel, n_meta=n_meta, lam_init=lam_init)
    return pl.pallas_call(
        kern,
        grid=(N_HEADS,),
        in_specs=[pl.BlockSpec(memory_space=pltpu.SMEM),
                  pl.BlockSpec((n_meta, HEAD_V), lambda h: (meta_blk, h)),
                  pl.BlockSpec((n_meta, HEAD_V), lambda h: (meta_blk, N_HEADS + h)),
                  pl.BlockSpec((n_meta, HEAD_V), lambda h: (meta_blk, 2 * N_HEADS + h)),
                  vec, vec, vec, vec,
                  pl.BlockSpec((1, HEAD_V), lambda h: (0, 0))],
        out_specs=pl.BlockSpec((n_meta, HEAD_V), lambda h: (0, h)),
        out_shape=jax.ShapeDtypeStruct((n_meta, D_ATTN), F32),
        compiler_params=_params(1),
        name="meta_attn",
    )(slopes, z_s, z_s, z_s, *lam_vecs, subln_g)


def _sample_attn_kernel(pt_ref, slope_ref, q_ref, kn_ref, vn_ref, *rest,
                        pages_per_step, page_size, ld, past, lam_init):
    del pt_ref
    P = pages_per_step
    k_pages = rest[:P]
    v_pages = rest[P:2 * P]
    lq1_ref, lk1_ref, lq2_ref, lk2_ref, g_ref, o_ref, q2_ref, m_ref, l_ref, acc_ref = rest[2 * P:]
    c = pl.program_id(1)
    tk = P * page_size
    hr = 2 * ld
    rows = N_HEADS * hr

    @pl.when(c == 0)
    def _():
        for h in range(N_HEADS):
            q2_ref[h] = _stack_maps(q_ref[:, h * HEAD_V:(h + 1) * HEAD_V])
        m_ref[...] = jnp.full(m_ref.shape, MASK_VALUE, F32)
        l_ref[...] = jnp.zeros(l_ref.shape, F32)
        acc_ref[...] = jnp.zeros(acc_ref.shape, F32)

    slope_col = jnp.concatenate([jnp.full((hr, 1), slope_ref[h], F32) for h in range(N_HEADS)], axis=0)

    def update(s, values):
        m_old = m_ref[...]
        m_new = jnp.maximum(m_old, jnp.max(s, axis=-1, keepdims=True))
        p = jnp.exp(s - m_new)
        alpha = jnp.exp(m_old - m_new)
        l_ref[...] = alpha * l_ref[...] + jnp.sum(p, axis=-1, keepdims=True)
        p = p.astype(BF16)
        pv = [jnp.dot(p[h * hr:(h + 1) * hr], values[h], preferred_element_type=F32)
              for h in range(N_HEADS)]
        acc_ref[...] = alpha * acc_ref[...] + jnp.concatenate(pv, axis=0)
        m_ref[...] = m_new

    row = lax.broadcasted_iota(jnp.int32, (rows, tk), 0)
    col = lax.broadcasted_iota(jnp.int32, (rows, tk), 1)
    rel = (c * tk + col - (past + lax.rem(row, ld))).astype(F32)
    s_parts, values = [], []
    for h in range(N_HEADS):
        kh = jnp.concatenate([kp[0, pl.ds(h, page_size, stride=N_HEADS), :] for kp in k_pages], axis=0)
        vh = jnp.concatenate([vp[0, pl.ds(h, page_size, stride=N_HEADS), :] for vp in v_pages], axis=0)
        s_parts.append(lax.dot_general(q2_ref[h], kh.astype(BF16), _NT, preferred_element_type=F32))
        values.append(vh.astype(BF16))
    update(jnp.concatenate(s_parts, axis=0) + slope_col * rel, values)

    @pl.when(c == pl.num_programs(1) - 1)
    def _():
        lam = _diff_lambda(lq1_ref, lk1_ref, lq2_ref, lk2_ref, lam_init)
        row = lax.broadcasted_iota(jnp.int32, (rows, hr), 0)
        col = lax.broadcasted_iota(jnp.int32, (rows, hr), 1)
        qi = lax.rem(row, ld)
        pad = jnp.zeros((hr - ld, HEAD_V), F32)
        s_parts, values = [], []
        for h in range(N_HEADS):
            kn = jnp.concatenate([kn_ref[:, h * HEAD_V:(h + 1) * HEAD_V], pad], axis=0).astype(BF16)
            vn = jnp.concatenate([vn_ref[:, h * HEAD_V:(h + 1) * HEAD_V], pad], axis=0).astype(BF16)
            s_parts.append(lax.dot_general(q2_ref[h], kn, _NT, preferred_element_type=F32))
            values.append(vn)
        s = jnp.concatenate(s_parts, axis=0) + slope_col * (col - qi).astype(F32)
        update(jnp.where(col <= qi, s, MASK_VALUE), values)
        acc = acc_ref[...]
        l = l_ref[...]
        for h in range(N_HEADS):
            o_ref[:, h * HEAD_V:(h + 1) * HEAD_V] = _diff_combine(
                acc[h * hr:(h + 1) * hr], l[h * hr:(h + 1) * hr], lam, g_ref[...], lam_init)


def _sample_attention(z_s, cache_k, cache_v, page_table, slopes, lam_vecs, subln_g, lam_init, ld):
    bd, n_pages = page_table.shape
    n_phys, page_size = cache_k.shape[0], cache_k.shape[1]
    past = n_pages * page_size
    P = max(p for p in range(1, 9) if n_pages % p == 0)
    k2 = cache_k.reshape(n_phys, page_size * N_HEADS, HEAD_V)
    v2 = cache_v.reshape(n_phys, page_size * N_HEADS, HEAD_V)
    vec = pl.BlockSpec((1, HEAD_QK), lambda b, c, pt: (0, 0))

    def page_spec(j):
        return pl.BlockSpec((1, page_size * N_HEADS, HEAD_V),
                            lambda b, c, pt: (pt[b * n_pages + c * P + j], 0, 0))

    kern = functools.partial(_sample_attn_kernel, pages_per_step=P, page_size=page_size, ld=ld,
                             past=past, lam_init=lam_init)
    grid_spec = pltpu.PrefetchScalarGridSpec(
        num_scalar_prefetch=1,
        grid=(bd, n_pages // P),
        in_specs=[pl.BlockSpec(memory_space=pltpu.SMEM),
                  pl.BlockSpec((ld, D_ATTN), lambda b, c, pt: (b, 0)),
                  pl.BlockSpec((ld, D_ATTN), lambda b, c, pt: (b, 1)),
                  pl.BlockSpec((ld, D_ATTN), lambda b, c, pt: (b, 2))]
                 + [page_spec(j) for j in range(P)] + [page_spec(j) for j in range(P)]
                 + [vec, vec, vec, vec, pl.BlockSpec((1, HEAD_V), lambda b, c, pt: (0, 0))],
        out_specs=pl.BlockSpec((ld, D_ATTN), lambda b, c, pt: (b, 0)),
        scratch_shapes=[pltpu.VMEM((N_HEADS, 2 * ld, HEAD_V), BF16),
                        pltpu.VMEM((N_HEADS * 2 * ld, 1), F32),
                        pltpu.VMEM((N_HEADS * 2 * ld, 1), F32),
                        pltpu.VMEM((N_HEADS * 2 * ld, HEAD_V), F32)])
    return pl.pallas_call(
        kern,
        grid_spec=grid_spec,
        out_shape=jax.ShapeDtypeStruct((bd * ld, D_ATTN), F32),
        compiler_params=_params(2),
        name="sample_attn",
    )(page_table.reshape(-1), slopes, z_s, z_s, z_s, *([k2] * P), *([v2] * P), *lam_vecs, subln_g)


def _window_sums(ext):
    out = {}
    s = ext
    w = 1
    while w < max(POOL_WINDOWS):
        s = s + pltpu.roll(s, w, 0)
        w *= 2
        out[w] = s
    return out


def _pool_project(u, sums, inv_cnt, wp_ref, scale_ref):
    cg = u.shape[1] // len(POOL_WINDOWS)
    outs = []
    for g, w in enumerate(POOL_WINDOWS):
        sl = slice(g * cg, (g + 1) * cg)
        d = sums[w][:, sl] * inv_cnt[w] - u[:, sl]
        outs.append(jnp.dot(d.astype(BF16), wp_ref[g], preferred_element_type=F32))
    return jnp.concatenate(outs, axis=1) * scale_ref[...]


def _pool_big_kernel(u_ref, prev_ref, meta_ref, wp_ref, scale_ref, m_ref):
    halo = jnp.where(pl.program_id(0) == 0, meta_ref[...], prev_ref[...])
    u = u_ref[...]
    sums = _window_sums(jnp.concatenate([halo, u], axis=0))
    sums = {w: s[POOL_HALO:] for w, s in sums.items()}
    inv = {w: 1.0 / w for w in POOL_WINDOWS}
    m_ref[...] = _pool_project(u, sums, inv, wp_ref, scale_ref).astype(m_ref.dtype)


def _pool_big(z_b, z_s, meta_row0, w_pool, pool_scale, tm=512):
    seq = z_b.shape[0]
    d_pool = w_pool.shape[0] * w_pool.shape[1]
    tm = _row_tile(seq, tm)
    ucol = z_b.shape[1] // d_pool - 1
    per = tm // POOL_HALO
    return pl.pallas_call(
        _pool_big_kernel,
        grid=(seq // tm,),
        in_specs=[pl.BlockSpec((tm, d_pool), lambda i: (i, ucol)),
                  pl.BlockSpec((POOL_HALO, d_pool), lambda i: (jnp.maximum(i * per - 1, 0), ucol)),
                  pl.BlockSpec((POOL_HALO, d_pool), lambda i: (meta_row0 // POOL_HALO, ucol)),
                  pl.BlockSpec(w_pool.shape, lambda i: (0, 0, 0)),
                  pl.BlockSpec((1, d_pool), lambda i: (0, 0))],
        out_specs=pl.BlockSpec((tm, d_pool), lambda i: (i, 0)),
        out_shape=jax.ShapeDtypeStruct((seq, d_pool), BF16),
        compiler_params=_params(1),
        name="pool_big",
    )(z_b, z_b, z_s, w_pool, pool_scale)


def _pool_small_kernel(ext_ref, wp_ref, scale_ref, m_ref, *, bd, ld, n_meta):
    grp = POOL_HALO + ld
    ext = ext_ref[...]
    sums = _window_sums(ext)
    meta0 = bd * grp + POOL_HALO

    def new_rows(a):
        parts = [a[b * grp + POOL_HALO:(b + 1) * grp] for b in range(bd)]
        return jnp.concatenate(parts + [a[meta0:meta0 + n_meta]], axis=0)

    n = bd * ld + n_meta
    r = lax.broadcasted_iota(jnp.int32, (n, 1), 0)
    inv = {}
    for w in POOL_WINDOWS:
        cnt = jnp.where(r < bd * ld, w, jnp.minimum(w, r - bd * ld + 1))
        inv[w] = 1.0 / cnt.astype(F32)
    sums = {w: new_rows(s) for w, s in sums.items()}
    m_ref[...] = _pool_project(new_rows(ext), sums, inv, wp_ref, scale_ref).astype(m_ref.dtype)


def _pool_small(ext, w_pool, pool_scale, bd, ld, n_meta):
    n = bd * ld + n_meta
    d_pool = ext.shape[1]
    kern = functools.partial(_pool_small_kernel, bd=bd, ld=ld, n_meta=n_meta)
    return pl.pallas_call(
        kern,
        grid=(1,),
        in_specs=[pl.BlockSpec(ext.shape, lambda i: (0, 0)),
                  pl.BlockSpec(w_pool.shape, lambda i: (0, 0, 0)),
                  pl.BlockSpec((1, d_pool), lambda i: (0, 0))],
        out_specs=pl.BlockSpec((n, d_pool), lambda i: (0, 0)),
        out_shape=jax.ShapeDtypeStruct((n, d_pool), BF16),
        compiler_params=_params(1),
        name="pool_small",
    )(ext, w_pool, pool_scale)


def _out_proj_kernel(x_ref, a_ref, m_ref, wa_ref, wm_ref, h_ref):
    h_ref[...] = (x_ref[...]
                  + jnp.dot(a_ref[...].astype(BF16), wa_ref[...], preferred_element_type=F32)
                  + jnp.dot(m_ref[...], wm_ref[...], preferred_element_type=F32))


def _out_proj(x, a, m, w_out, tm=512):
    rows, d = x.shape
    da, dm = a.shape[1], m.shape[1]
    tm = _row_tile(rows, tm)
    return pl.pallas_call(
        _out_proj_kernel,
        grid=(rows // tm,),
        in_specs=[pl.BlockSpec((tm, d), lambda i: (i, 0)),
                  pl.BlockSpec((tm, da), lambda i: (i, 0)),
                  pl.BlockSpec((tm, dm), lambda i: (i, 0)),
                  pl.BlockSpec((da, d), lambda i: (0, 0)),
                  pl.BlockSpec((dm, d), lambda i: (da // dm, 0))],
        out_specs=pl.BlockSpec((tm, d), lambda i: (i, 0)),
        out_shape=jax.ShapeDtypeStruct((rows, d), F32),
        compiler_params=_params(1),
        name="out_proj",
    )(x, a, m, w_out, w_out)


def _mlp_kernel(h_ref, g_ref, wu_ref, wd_ref, gf_ref, y_ref, xn_ref):
    j = pl.program_id(1)

    @pl.when(j == 0)
    def _():
        h = h_ref[...]
        xn_ref[...] = _rmsnorm(h, g_ref[...]).astype(BF16)
        y_ref[...] = h

    a = jnp.maximum(jnp.dot(xn_ref[...], wu_ref[...], preferred_element_type=F32), 0.0)
    y_ref[...] += jnp.dot((a * a).astype(BF16), wd_ref[...], preferred_element_type=F32)

    @pl.when(j == pl.num_programs(1) - 1)
    def _():
        y_ref[...] = _rmsnorm(y_ref[...], gf_ref[...])


def _mlp(h, g, w_up, w_down, g_final, tm=512, tf=1024):
    rows, d = h.shape
    d_ff = w_up.shape[1]
    tm = _row_tile(rows, tm)
    return pl.pallas_call(
        _mlp_kernel,
        grid=(rows // tm, d_ff // tf),
        in_specs=[pl.BlockSpec((tm, d), lambda i, j: (i, 0)),
                  pl.BlockSpec((1, d), lambda i, j: (0, 0)),
                  pl.BlockSpec((d, tf), lambda i, j: (0, j)),
                  pl.BlockSpec((tf, d), lambda i, j: (j, 0)),
                  pl.BlockSpec((1, d), lambda i, j: (0, 0))],
        out_specs=pl.BlockSpec((tm, d), lambda i, j: (i, 0)),
        out_shape=jax.ShapeDtypeStruct((rows, d), F32),
        scratch_shapes=[pltpu.VMEM((tm, d), BF16)],
        compiler_params=_params(2),
        name="mlp",
    )(h, g, w_up, w_down, g_final)


def kernel(x_prompt, x_sample, cache_k, cache_v, state_pool, page_table, meta_tokens, norm_mix_g,
           w_in, lambda_q1, lambda_k1, lambda_q2, lambda_k2, subln_g, w_pool, pool_scale, w_out,
           norm_mlp_g, w_up, w_down, norm_final_g):
    depth = norm_mix_g.shape[0]
    batch, seq, d_model = x_prompt.shape
    bd, ld, _ = x_sample.shape
    n_meta = meta_tokens.shape[0]
    assert depth == 1 and batch == 1, "one layer and one prompt sequence are supported"
    assert n_meta == POOL_HALO and (bd * ld) % n_meta == 0 and seq >= POOL_HALO
    layer = 0
    lam_init = 0.8 - 0.6 * math.exp(-0.3 * layer)
    slopes = 2.0 ** (-8.0 * jnp.arange(1, N_HEADS + 1, dtype=F32) / N_HEADS)
    n_small = bd * ld
    d_pool = w_pool.shape[1] * w_pool.shape[2]

    x_big = x_prompt.reshape(seq, d_model)
    x_small = jnp.concatenate([x_sample.reshape(n_small, d_model), meta_tokens.astype(F32)], axis=0)

    g_mix = norm_mix_g[layer].reshape(1, d_model)
    g_mlp = norm_mlp_g[layer].reshape(1, d_model)
    g_final = norm_final_g.reshape(1, d_model)
    g_sub = subln_g[layer].reshape(1, HEAD_V)
    scale_pool = pool_scale[layer].reshape(1, d_pool)
    lam_vecs = [v[layer].reshape(1, HEAD_QK) for v in (lambda_q1, lambda_k1, lambda_q2, lambda_k2)]
    w_in_b = w_in[layer].astype(BF16)
    w_out_b = w_out[layer].astype(BF16)
    w_up_b = w_up[layer].astype(BF16)
    w_down_b = w_down[layer].astype(BF16)
    w_pool_b = w_pool[layer].astype(BF16)

    z_b = _in_proj(x_big, g_mix, w_in_b)
    z_s = _in_proj(x_small, g_mix, w_in_b)

    a_b = _prompt_attention(z_b, z_s, n_small, slopes, lam_vecs, g_sub, lam_init, n_meta)
    a_meta = _meta_attention(z_s, n_small, slopes, lam_vecs, g_sub, lam_init, n_meta)
    a_samp = _sample_attention(z_s, cache_k[layer], cache_v[layer], page_table, slopes, lam_vecs,
                               g_sub, lam_init, ld)
    a_s = jnp.concatenate([a_samp, a_meta], axis=0)

    u_b = z_b[:, 3 * D_ATTN:]
    u_s = z_s[:n_small, 3 * D_ATTN:].reshape(bd, ld, d_pool)
    u_meta = z_s[n_small:, 3 * D_ATTN:]
    hist = state_pool[layer].astype(F32)
    n_hist = hist.shape[1]
    zpad = jnp.zeros((bd, POOL_HALO - n_hist, d_pool), F32)
    ext = jnp.concatenate([jnp.concatenate([zpad, hist, u_s], axis=1).reshape(-1, d_pool),
                           jnp.zeros((POOL_HALO, d_pool), F32), u_meta], axis=0)
    m_b = _pool_big(z_b, z_s, n_small, w_pool_b, scale_pool)
    m_s = _pool_small(ext, w_pool_b, scale_pool, bd, ld, n_meta)

    h_b = _out_proj(x_big, a_b, m_b, w_out_b)
    h_s = _out_proj(x_small, a_s, m_s, w_out_b)
    y_b = _mlp(h_b, g_mlp, w_up_b, w_down_b, g_final)
    y_s = _mlp(h_s, g_mlp, w_up_b, w_down_b, g_final)

    def heads(z, col):
        return z[:, col * D_ATTN:(col + 1) * D_ATTN].reshape(-1, N_HEADS, HEAD_V)

    t = seq + n_meta
    k_prompt = jnp.concatenate([heads(z_s[n_small:], 1), heads(z_b, 1)], axis=0)
    v_prompt = jnp.concatenate([heads(z_s[n_small:], 2), heads(z_b, 2)], axis=0)
    u_ext = jnp.concatenate([hist, u_s], axis=1)
    return (y_b.reshape(1, seq, d_model),
            y_s[:n_small].reshape(bd, ld, d_model),
            k_prompt.reshape(1, 1, t, N_HEADS, HEAD_V),
            v_prompt.reshape(1, 1, t, N_HEADS, HEAD_V),
            u_b[-n_hist:].reshape(1, 1, n_hist, d_pool),
            heads(z_s[:n_small], 1).reshape(1, bd, ld, N_HEADS, HEAD_V),
            heads(z_s[:n_small], 2).reshape(1, bd, ld, N_HEADS, HEAD_V),
            u_ext[:, -n_hist:].reshape(1, bd, n_hist, d_pool))
```

```python
import functools
import math

import jax
import jax.numpy as jnp
from jax import lax
from jax.experimental import pallas as pl
from jax.experimental.pallas import tpu as pltpu

N_HEADS = 8
HEAD_V = 128
HEAD_QK = HEAD_V // 2
D_ATTN = N_HEADS * HEAD_V
POOL_WINDOWS = (2, 4, 8, 16)
POOL_HALO = 16
EPS = 1e-6
MASK_VALUE = -1e30
QK_SCALE = HEAD_QK ** -0.5
LOG2E = 1.4426950408889634

VMEM_LIMIT_BYTES = 56 * 1024 * 1024
MAX_PAGES_PER_STEP = 16
BF16 = jnp.bfloat16
F32 = jnp.float32

_NT = (((1,), (1,)), ((), ()))


def _params(n_grid_dims):
    return pltpu.CompilerParams(dimension_semantics=("arbitrary",) * n_grid_dims,
                                vmem_limit_bytes=VMEM_LIMIT_BYTES)


def _row_tile(rows, target):
    best = rows
    for t in range(16, min(rows, target) + 1, 16):
        if rows % t == 0:
            best = t
    return best if best <= target else rows


def _rmsnorm(x, g):
    ms = jnp.mean(x * x, axis=-1, keepdims=True)
    return x * lax.rsqrt(ms + EPS) * g


def _in_proj_kernel(x_ref, g_ref, w_ref, z_ref, xn_ref):
    @pl.when(pl.program_id(1) == 0)
    def _():
        xn_ref[...] = _rmsnorm(x_ref[...], g_ref[...]).astype(BF16)

    z_ref[...] = jnp.dot(xn_ref[...], w_ref[...], preferred_element_type=F32)


def _in_proj(x, g, w, tm_target=512, tn=2048):
    rows, d = x.shape
    n = w.shape[1]
    tm = _row_tile(rows, tm_target)
    return pl.pallas_call(
        _in_proj_kernel,
        grid=(rows // tm, n // tn),
        in_specs=[pl.BlockSpec((tm, d), lambda i, j: (i, 0)),
                  pl.BlockSpec((1, d), lambda i, j: (0, 0)),
                  pl.BlockSpec((d, tn), lambda i, j: (0, j))],
        out_specs=pl.BlockSpec((tm, tn), lambda i, j: (i, j)),
        out_shape=jax.ShapeDtypeStruct((rows, n), F32),
        scratch_shapes=[pltpu.VMEM((tm, d), BF16)],
        compiler_params=_params(2),
        name="in_proj",
    )(x, g, w)


def _stack_maps(q, scale=QK_SCALE):
    q = q * scale
    lane = lax.broadcasted_iota(jnp.int32, q.shape, 1)
    q1 = jnp.where(lane < HEAD_QK, q, 0.0)
    q2 = jnp.where(lane >= HEAD_QK, q, 0.0)
    return jnp.concatenate([q1, q2], axis=0).astype(BF16)


def _diff_lambda(lq1_ref, lk1_ref, lq2_ref, lk2_ref, lam_init):
    a = jnp.sum(lq1_ref[...] * lk1_ref[...], axis=-1, keepdims=True)
    b = jnp.sum(lq2_ref[...] * lk2_ref[...], axis=-1, keepdims=True)
    return jnp.exp(a) - jnp.exp(b) + lam_init


def _diff_combine(acc, l, lam, g, lam_init):
    n = acc.shape[0] // 2
    o = acc[:n] / l[:n] - lam * (acc[n:] / l[n:])
    return _rmsnorm(o, g) * (1.0 - lam_init)


def _prompt_attn_kernel(slope_ref, q_ref, k_ref, v_ref, km_ref, vm_ref,
                        lq1_ref, lk1_ref, lq2_ref, lk2_ref, gcol_ref, o_ref,
                        kb_ref, vt_ref, kmb_ref, vmt_ref, q2_ref, m_ref, l_ref, acc_ref,
                        bias_ref, dbias_ref, sa_ref, sb_ref, *, tq, n_meta, lam_init):
    h = pl.program_id(0)
    i = pl.program_id(1)
    slope = slope_ref[h] * LOG2E
    n_chunks = kb_ref.shape[0]

    @pl.when(i == 0)
    def _():
        for c in range(n_chunks):
            kb_ref[c] = k_ref[c * tq:(c + 1) * tq, :].astype(BF16)
            vt_ref[c] = v_ref[c * tq:(c + 1) * tq, :].T.astype(BF16)
        pad = jnp.zeros((HEAD_V - n_meta, HEAD_V), F32)
        kmb_ref[...] = jnp.concatenate([km_ref[...], pad], axis=0).astype(BF16)
        vmt_ref[...] = jnp.concatenate([vm_ref[...], pad], axis=0).T.astype(BF16)
        key = lax.broadcasted_iota(jnp.int32, (tq, 2 * tq), 0)
        qry = lax.broadcasted_iota(jnp.int32, (tq, 2 * tq), 1)
        qry = jnp.where(qry >= tq, qry - tq, qry)
        b = slope * (key - qry).astype(F32)
        bias_ref[...] = b
        dbias_ref[...] = jnp.where(key <= qry, b, MASK_VALUE)

    q2_ref[...] = _stack_maps(q_ref[...], QK_SCALE * LOG2E)

    def scores(c, s_ref):
        s_ref[...] = lax.dot_general(kb_ref[c], q2_ref[...], _NT, preferred_element_type=F32)

    scores(0, sa_ref)

    key = lax.broadcasted_iota(jnp.int32, (HEAD_V, 2 * tq), 0)
    qry = lax.broadcasted_iota(jnp.int32, (HEAD_V, 2 * tq), 1)
    qry = jnp.where(qry >= tq, qry - tq, qry)
    qpos = n_meta + i * tq + qry
    s = lax.dot_general(kmb_ref[...], q2_ref[...], _NT, preferred_element_type=F32)
    s = jnp.where(key < n_meta, s + slope * (key - qpos).astype(F32), MASK_VALUE)
    m0 = jnp.max(s, axis=0, keepdims=True)
    p = jnp.exp2(s - m0)
    m_ref[...] = m0
    l_ref[...] = jnp.sum(p, axis=0, keepdims=True)
    acc_ref[...] = jnp.dot(vmt_ref[...], p.astype(BF16), preferred_element_type=F32)

    def softmax_update(c, s_ref, b_ref):
        s = s_ref[...] + b_ref[...]
        sigma = slope * (tq * (c - i)).astype(F32)
        m_old = m_ref[...]
        m_new = jnp.maximum(m_old, jnp.max(s, axis=0, keepdims=True) + sigma)
        p = jnp.exp2(s - (m_new - sigma))
        alpha = jnp.exp2(m_old - m_new)
        l_ref[...] = alpha * l_ref[...] + jnp.sum(p, axis=0, keepdims=True)
        acc_ref[...] = alpha * acc_ref[...] + jnp.dot(vt_ref[c], p.astype(BF16),
                                                      preferred_element_type=F32)
        m_ref[...] = m_new

    def pair(j, carry):
        c = 2 * j
        scores(c + 1, sb_ref)
        softmax_update(c, sa_ref, bias_ref)
        scores(c + 2, sa_ref)
        softmax_update(c + 1, sb_ref, bias_ref)
        return carry

    lax.fori_loop(0, lax.shift_right_logical(i, 1), pair, 0)

    @pl.when((i & 1) == 0)
    def _():
        softmax_update(i, sa_ref, dbias_ref)

    @pl.when((i & 1) == 1)
    def _():
        scores(i, sb_ref)
        softmax_update(i - 1, sa_ref, bias_ref)
        softmax_update(i, sb_ref, dbias_ref)

    lam = _diff_lambda(lq1_ref, lk1_ref, lq2_ref, lk2_ref, lam_init)
    acc = acc_ref[...]
    l = l_ref[...]
    o = acc[:, :tq] / l[:, :tq] - lam * (acc[:, tq:] / l[:, tq:])
    ms = jnp.mean(o * o, axis=0, keepdims=True)
    o = o * lax.rsqrt(ms + EPS) * gcol_ref[...] * (1.0 - lam_init)
    o_ref[...] = o.T.astype(o_ref.dtype)


def _prompt_attention(z_b, z_s, meta_row0, slopes, lam_vecs, subln_g, lam_init, n_meta, tq=512):
    seq = z_b.shape[0]
    tq = _row_tile(seq, tq)
    assert tq % 128 == 0 and meta_row0 % n_meta == 0 and n_meta <= HEAD_V
    meta_blk = meta_row0 // n_meta
    vec = pl.BlockSpec((1, HEAD_QK), lambda h, i: (0, 0))
    kern = functools.partial(_prompt_attn_kernel, tq=tq, n_meta=n_meta, lam_init=lam_init)
    return pl.pallas_call(
        kern,
        grid=(N_HEADS, seq // tq),
        in_specs=[pl.BlockSpec(memory_space=pltpu.SMEM),
                  pl.BlockSpec((tq, HEAD_V), lambda h, i: (i, h)),
                  pl.BlockSpec((seq, HEAD_V), lambda h, i: (0, N_HEADS + h)),
                  pl.BlockSpec((seq, HEAD_V), lambda h, i: (0, 2 * N_HEADS + h)),
                  pl.BlockSpec((n_meta, HEAD_V), lambda h, i: (meta_blk, N_HEADS + h)),
                  pl.BlockSpec((n_meta, HEAD_V), lambda h, i: (meta_blk, 2 * N_HEADS + h)),
                  vec, vec, vec, vec,
                  pl.BlockSpec((HEAD_V, 1), lambda h, i: (0, 0))],
        out_specs=pl.BlockSpec((tq, HEAD_V), lambda h, i: (i, h)),
        out_shape=jax.ShapeDtypeStruct((seq, D_ATTN), BF16),
        scratch_shapes=[pltpu.VMEM((seq // tq, tq, HEAD_V), BF16),
                        pltpu.VMEM((seq // tq, HEAD_V, tq), BF16),
                        pltpu.VMEM((HEAD_V, HEAD_V), BF16),
                        pltpu.VMEM((HEAD_V, HEAD_V), BF16),
                        pltpu.VMEM((2 * tq, HEAD_V), BF16),
                        pltpu.VMEM((1, 2 * tq), F32),
                        pltpu.VMEM((1, 2 * tq), F32),
                        pltpu.VMEM((HEAD_V, 2 * tq), F32),
                        pltpu.VMEM((tq, 2 * tq), F32),
                        pltpu.VMEM((tq, 2 * tq), F32),
                        pltpu.VMEM((tq, 2 * tq), F32),
                        pltpu.VMEM((tq, 2 * tq), F32)],
        compiler_params=_params(2),
        name="prompt_attn",
    )(slopes, z_b, z_b, z_b, z_s, z_s, *lam_vecs, subln_g.reshape(HEAD_V, 1))


def _meta_attn_kernel(slope_ref, q_ref, k_ref, v_ref, lq1_ref, lk1_ref, lq2_ref, lk2_ref, g_ref,
                      o_ref, *, n_meta, lam_init):
    slope = slope_ref[pl.program_id(0)]
    q2 = _stack_maps(q_ref[...])
    row = lax.broadcasted_iota(jnp.int32, (2 * n_meta, n_meta), 0)
    col = lax.broadcasted_iota(jnp.int32, (2 * n_meta, n_meta), 1)
    row = jnp.where(row >= n_meta, row - n_meta, row)
    s = lax.dot_general(q2, k_ref[...].astype(BF16), _NT, preferred_element_type=F32)
    s = jnp.where(col <= row, s + slope * (col - row).astype(F32), MASK_VALUE)
    p = jnp.exp(s - jnp.max(s, axis=-1, keepdims=True))
    l = jnp.sum(p, axis=-1, keepdims=True)
    acc = jnp.dot(p.astype(BF16), v_ref[...].astype(BF16), preferred_element_type=F32)
    lam = _diff_lambda(lq1_ref, lk1_ref, lq2_ref, lk2_ref, lam_init)
    o_ref[...] = _diff_combine(acc, l, lam, g_ref[...], lam_init)


def _meta_attention(z_s, meta_row0, slopes, lam_vecs, subln_g, lam_init, n_meta):
    meta_blk = meta_row0 // n_meta
    vec = pl.BlockSpec((1, HEAD_QK), lambda h: (0, 0))
    kern = functools.partial(_meta_attn_kernel, n_meta=n_meta, lam_init=lam_init)
    return pl.pallas_call(
        kern,
        grid=(N_HEADS,),
        in_specs=[pl.BlockSpec(memory_space=pltpu.SMEM),
                  pl.BlockSpec((n_meta, HEAD_V), lambda h: (meta_blk, h)),
                  pl.BlockSpec((n_meta, HEAD_V), lambda h: (meta_blk, N_HEADS + h)),
                  pl.BlockSpec((n_meta, HEAD_V), lambda h: (meta_blk, 2 * N_HEADS + h)),
                  vec, vec, vec, vec,
                  pl.BlockSpec((1, HEAD_V), lambda h: (0, 0))],
        out_specs=pl.BlockSpec((n_meta, HEAD_V), lambda h: (0, h)),
        out_shape=jax.ShapeDtypeStruct((n_meta, D_ATTN), F32),
        compiler_params=_params(1),
        name="meta_attn",
    )(slopes, z_s, z_s, z_s, *lam_vecs, subln_g)


def _sample_attn_kernel(pt_ref, slope_ref, q_ref, kn_ref, vn_ref, *rest,
                        pages_per_step, page_size, ld, past, lam_init):
    del pt_ref
    P = pages_per_step
    k_pages = rest[:P]
    v_pages = rest[P:2 * P]
    (lq1_ref, lk1_ref, lq2_ref, lk2_ref, g_ref, o_ref,
     q2_ref, m_ref, l_ref, acc_ref, bias_ref) = rest[2 * P:]
    c = pl.program_id(1)
    tk = P * page_size
    n_pairs = N_HEADS // 2
    hr = 2 * ld
    pr = 2 * hr
    rows = N_HEADS * hr
    pair_heads = [(j, j + n_pairs) for j in range(n_pairs)]

    slope_col = jnp.concatenate([jnp.full((hr, 1), slope_ref[h], F32)
                                 for pair in pair_heads for h in pair], axis=0)

    def row_head_and_query(shape):
        row = lax.broadcasted_iota(jnp.int32, shape, 0)
        return lax.rem(lax.div(row, hr), 2), lax.rem(row, ld)

    @pl.when((pl.program_id(0) == 0) & (c == 0))
    def _():
        rh, qi = row_head_and_query((rows, 2 * tk))
        col = lax.broadcasted_iota(jnp.int32, (rows, 2 * tk), 1)
        rel = (lax.div(col, 2) - qi).astype(F32)
        bias_ref[...] = jnp.where(lax.rem(col, 2) == rh, slope_col * rel, MASK_VALUE)

    @pl.when(c == 0)
    def _():
        for j, pair in enumerate(pair_heads):
            q2_ref[j] = jnp.concatenate(
                [_stack_maps(q_ref[:, h * HEAD_V:(h + 1) * HEAD_V]) for h in pair], axis=0)
        m_ref[...] = jnp.full(m_ref.shape, MASK_VALUE, F32)
        l_ref[...] = jnp.zeros(l_ref.shape, F32)
        acc_ref[...] = jnp.zeros(acc_ref.shape, F32)

    def update(s, values):
        m_old = m_ref[...]
        m_new = jnp.maximum(m_old, jnp.max(s, axis=-1, keepdims=True))
        p = jnp.exp(s - m_new)
        alpha = jnp.exp(m_old - m_new)
        l_ref[...] = alpha * l_ref[...] + jnp.sum(p, axis=-1, keepdims=True)
        p = p.astype(BF16)
        pv = [jnp.dot(p[j * pr:(j + 1) * pr], values[j], preferred_element_type=F32)
              for j in range(n_pairs)]
        acc_ref[...] = alpha * acc_ref[...] + jnp.concatenate(pv, axis=0)
        m_ref[...] = m_new

    def pair_rows(pages, j):
        return jnp.concatenate([pg[0, pl.ds(j, 2 * page_size, stride=n_pairs), :] for pg in pages],
                               axis=0).astype(BF16)

    s_parts, values = [], []
    for j in range(n_pairs):
        s_parts.append(lax.dot_general(q2_ref[j], pair_rows(k_pages, j), _NT,
                                       preferred_element_type=F32))
        values.append(pair_rows(v_pages, j))
    group_offset = slope_col * (c * tk - past).astype(F32)
    update(jnp.concatenate(s_parts, axis=0) + bias_ref[...] + group_offset, values)

    @pl.when(c == pl.num_programs(1) - 1)
    def _():
        lam = _diff_lambda(lq1_ref, lk1_ref, lq2_ref, lk2_ref, lam_init)
        rh, qi = row_head_and_query((rows, hr))
        col = lax.broadcasted_iota(jnp.int32, (rows, hr), 1)
        kj = lax.rem(col, ld)
        visible = (lax.div(col, ld) == rh) & (kj <= qi)
        s_parts, values = [], []
        for j, pair in enumerate(pair_heads):
            kn = jnp.concatenate([kn_ref[:, h * HEAD_V:(h + 1) * HEAD_V] for h in pair], axis=0)
            vn = jnp.concatenate([vn_ref[:, h * HEAD_V:(h + 1) * HEAD_V] for h in pair], axis=0)
            s_parts.append(lax.dot_general(q2_ref[j], kn.astype(BF16), _NT,
                                           preferred_element_type=F32))
            values.append(vn.astype(BF16))
        s = jnp.concatenate(s_parts, axis=0) + slope_col * (kj - qi).astype(F32)
        update(jnp.where(visible, s, MASK_VALUE), values)
        acc = acc_ref[...]
        l = l_ref[...]
        for j, pair in enumerate(pair_heads):
            for t, h in enumerate(pair):
                r0 = j * pr + t * hr
                o_ref[:, h * HEAD_V:(h + 1) * HEAD_V] = _diff_combine(
                    acc[r0:r0 + hr], l[r0:r0 + hr], lam, g_ref[...], lam_init)


def _sample_attention(z_s, cache_k, cache_v, page_table, slopes, lam_vecs, subln_g, lam_init, ld):
    bd, n_pages = page_table.shape
    n_phys, page_size = cache_k.shape[0], cache_k.shape[1]
    past = n_pages * page_size
    P = max(p for p in range(1, MAX_PAGES_PER_STEP + 1) if n_pages % p == 0)
    k2 = cache_k.reshape(n_phys, page_size * N_HEADS, HEAD_V)
    v2 = cache_v.reshape(n_phys, page_size * N_HEADS, HEAD_V)
    vec = pl.BlockSpec((1, HEAD_QK), lambda b, c, pt: (0, 0))

    def page_spec(j):
        return pl.BlockSpec((1, page_size * N_HEADS, HEAD_V),
                            lambda b, c, pt: (pt[b * n_pages + c * P + j], 0, 0))

    kern = functools.partial(_sample_attn_kernel, pages_per_step=P, page_size=page_size, ld=ld,
                             past=past, lam_init=lam_init)
    grid_spec = pltpu.PrefetchScalarGridSpec(
        num_scalar_prefetch=1,
        grid=(bd, n_pages // P),
        in_specs=[pl.BlockSpec(memory_space=pltpu.SMEM),
                  pl.BlockSpec((ld, D_ATTN), lambda b, c, pt: (b, 0)),
                  pl.BlockSpec((ld, D_ATTN), lambda b, c, pt: (b, 1)),
                  pl.BlockSpec((ld, D_ATTN), lambda b, c, pt: (b, 2))]
                 + [page_spec(j) for j in range(P)] + [page_spec(j) for j in range(P)]
                 + [vec, vec, vec, vec, pl.BlockSpec((1, HEAD_V), lambda b, c, pt: (0, 0))],
        out_specs=pl.BlockSpec((ld, D_ATTN), lambda b, c, pt: (b, 0)),
        scratch_shapes=[pltpu.VMEM((N_HEADS // 2, 4 * ld, HEAD_V), BF16),
                        pltpu.VMEM((N_HEADS * 2 * ld, 1), F32),
                        pltpu.VMEM((N_HEADS * 2 * ld, 1), F32),
                        pltpu.VMEM((N_HEADS * 2 * ld, HEAD_V), F32),
                        pltpu.VMEM((N_HEADS * 2 * ld, 2 * P * page_size), F32)])
    return pl.pallas_call(
        kern,
        grid_spec=grid_spec,
        out_shape=jax.ShapeDtypeStruct((bd * ld, D_ATTN), F32),
        compiler_params=_params(2),
        name="sample_attn",
    )(page_table.reshape(-1), slopes, z_s, z_s, z_s, *([k2] * P), *([v2] * P), *lam_vecs, subln_g)


def _window_sums(ext):
    out = {}
    s = ext
    w = 1
    while w < max(POOL_WINDOWS):
        s = s + pltpu.roll(s, w, 0)
        w *= 2
        out[w] = s
    return out


def _pool_project(u, sums, inv_cnt, wp_ref, scale_ref):
    cg = u.shape[1] // len(POOL_WINDOWS)
    outs = []
    for g, w in enumerate(POOL_WINDOWS):
        sl = slice(g * cg, (g + 1) * cg)
        d = sums[w][:, sl] * inv_cnt[w] - u[:, sl]
        outs.append(jnp.dot(d.astype(BF16), wp_ref[g], preferred_element_type=F32))
    return jnp.concatenate(outs, axis=1) * scale_ref[...]


def _pool_big_kernel(u_ref, prev_ref, meta_ref, wp_ref, scale_ref, m_ref):
    halo = jnp.where(pl.program_id(0) == 0, meta_ref[...], prev_ref[...])
    u = u_ref[...]
    sums = _window_sums(jnp.concatenate([halo, u], axis=0))
    sums = {w: s[POOL_HALO:] for w, s in sums.items()}
    inv = {w: 1.0 / w for w in POOL_WINDOWS}
    m_ref[...] = _pool_project(u, sums, inv, wp_ref, scale_ref).astype(m_ref.dtype)


def _pool_big(z_b, z_s, meta_row0, w_pool, pool_scale, tm=512):
    seq = z_b.shape[0]
    d_pool = w_pool.shape[0] * w_pool.shape[1]
    tm = _row_tile(seq, tm)
    ucol = z_b.shape[1] // d_pool - 1
    per = tm // POOL_HALO
    return pl.pallas_call(
        _pool_big_kernel,
        grid=(seq // tm,),
        in_specs=[pl.BlockSpec((tm, d_pool), lambda i: (i, ucol)),
                  pl.BlockSpec((POOL_HALO, d_pool), lambda i: (jnp.maximum(i * per - 1, 0), ucol)),
                  pl.BlockSpec((POOL_HALO, d_pool), lambda i: (meta_row0 // POOL_HALO, ucol)),
                  pl.BlockSpec(w_pool.shape, lambda i: (0, 0, 0)),
                  pl.BlockSpec((1, d_pool), lambda i: (0, 0))],
        out_specs=pl.BlockSpec((tm, d_pool), lambda i: (i, 0)),
        out_shape=jax.ShapeDtypeStruct((seq, d_pool), BF16),
        compiler_params=_params(1),
        name="pool_big",
    )(z_b, z_b, z_s, w_pool, pool_scale)


def _pool_small_kernel(ext_ref, wp_ref, scale_ref, m_ref, *, bd, ld, n_meta):
    grp = POOL_HALO + ld
    ext = ext_ref[...]
    sums = _window_sums(ext)
    meta0 = bd * grp + POOL_HALO

    def new_rows(a):
        parts = [a[b * grp + POOL_HALO:(b + 1) * grp] for b in range(bd)]
        return jnp.concatenate(parts + [a[meta0:meta0 + n_meta]], axis=0)

    n = bd * ld + n_meta
    r = lax.broadcasted_iota(jnp.int32, (n, 1), 0)
    inv = {}
    for w in POOL_WINDOWS:
        cnt = jnp.where(r < bd * ld, w, jnp.minimum(w, r - bd * ld + 1))
        inv[w] = 1.0 / cnt.astype(F32)
    sums = {w: new_rows(s) for w, s in sums.items()}
    m_ref[...] = _pool_project(new_rows(ext), sums, inv, wp_ref, scale_ref).astype(m_ref.dtype)


def _pool_small(ext, w_pool, pool_scale, bd, ld, n_meta):
    n = bd * ld + n_meta
    d_pool = ext.shape[1]
    kern = functools.partial(_pool_small_kernel, bd=bd, ld=ld, n_meta=n_meta)
    return pl.pallas_call(
        kern,
        grid=(1,),
        in_specs=[pl.BlockSpec(ext.shape, lambda i: (0, 0)),
                  pl.BlockSpec(w_pool.shape, lambda i: (0, 0, 0)),
                  pl.BlockSpec((1, d_pool), lambda i: (0, 0))],
        out_specs=pl.BlockSpec((n, d_pool), lambda i: (0, 0)),
        out_shape=jax.ShapeDtypeStruct((n, d_pool), BF16),
        compiler_params=_params(1),
        name="pool_small",
    )(ext, w_pool, pool_scale)


def _out_proj_kernel(x_ref, a_ref, m_ref, wa_ref, wm_ref, h_ref):
    h_ref[...] = (x_ref[...]
                  + jnp.dot(a_ref[...].astype(BF16), wa_ref[...], preferred_element_type=F32)
                  + jnp.dot(m_ref[...], wm_ref[...], preferred_element_type=F32))


def _out_proj(x, a, m, w_out, tm=512):
    rows, d = x.shape
    da, dm = a.shape[1], m.shape[1]
    tm = _row_tile(rows, tm)
    return pl.pallas_call(
        _out_proj_kernel,
        grid=(rows // tm,),
        in_specs=[pl.BlockSpec((tm, d), lambda i: (i, 0)),
                  pl.BlockSpec((tm, da), lambda i: (i, 0)),
                  pl.BlockSpec((tm, dm), lambda i: (i, 0)),
                  pl.BlockSpec((da, d), lambda i: (0, 0)),
                  pl.BlockSpec((dm, d), lambda i: (da // dm, 0))],
        out_specs=pl.BlockSpec((tm, d), lambda i: (i, 0)),
        out_shape=jax.ShapeDtypeStruct((rows, d), F32),
        compiler_params=_params(1),
        name="out_proj",
    )(x, a, m, w_out, w_out)


def _mlp_kernel(h_ref, g_ref, wu_ref, wd_ref, gf_ref, y_ref, xn_ref):
    j = pl.program_id(1)

    @pl.when(j == 0)
    def _():
        h = h_ref[...]
        xn_ref[...] = _rmsnorm(h, g_ref[...]).astype(BF16)
        y_ref[...] = h

    a = jnp.maximum(jnp.dot(xn_ref[...], wu_ref[...], preferred_element_type=F32), 0.0)
    y_ref[...] += jnp.dot((a * a).astype(BF16), wd_ref[...], preferred_element_type=F32)

    @pl.when(j == pl.num_programs(1) - 1)
    def _():
        y_ref[...] = _rmsnorm(y_ref[...], gf_ref[...])


def _mlp(h, g, w_up, w_down, g_final, tm=512, tf=1024):
    rows, d = h.shape
    d_ff = w_up.shape[1]
    tm = _row_tile(rows, tm)
    return pl.pallas_call(
        _mlp_kernel,
        grid=(rows // tm, d_ff // tf),
        in_specs=[pl.BlockSpec((tm, d), lambda i, j: (i, 0)),
                  pl.BlockSpec((1, d), lambda i, j: (0, 0)),
                  pl.BlockSpec((d, tf), lambda i, j: (0, j)),
                  pl.BlockSpec((tf, d), lambda i, j: (j, 0)),
                  pl.BlockSpec((1, d), lambda i, j: (0, 0))],
        out_specs=pl.BlockSpec((tm, d), lambda i, j: (i, 0)),
        out_shape=jax.ShapeDtypeStruct((rows, d), F32),
        scratch_shapes=[pltpu.VMEM((tm, d), BF16)],
        compiler_params=_params(2),
        name="mlp",
    )(h, g, w_up, w_down, g_final)


def kernel(x_prompt, x_sample, cache_k, cache_v, state_pool, page_table, meta_tokens, norm_mix_g,
           w_in, lambda_q1, lambda_k1, lambda_q2, lambda_k2, subln_g, w_pool, pool_scale, w_out,
           norm_mlp_g, w_up, w_down, norm_final_g):
    depth = norm_mix_g.shape[0]
    batch, seq, d_model = x_prompt.shape
    bd, ld, _ = x_sample.shape
    n_meta = meta_tokens.shape[0]
    assert depth == 1 and batch == 1, "one layer and one prompt sequence are supported"
    assert n_meta == POOL_HALO and (bd * ld) % n_meta == 0 and seq >= POOL_HALO
    layer = 0
    lam_init = 0.8 - 0.6 * math.exp(-0.3 * layer)
    slopes = 2.0 ** (-8.0 * jnp.arange(1, N_HEADS + 1, dtype=F32) / N_HEADS)
    n_small = bd * ld
    d_pool = w_pool.shape[1] * w_pool.shape[2]

    x_big = x_prompt.reshape(seq, d_model)
    x_small = jnp.concatenate([x_sample.reshape(n_small, d_model), meta_tokens.astype(F32)], axis=0)

    g_mix = norm_mix_g[layer].reshape(1, d_model)
    g_mlp = norm_mlp_g[layer].reshape(1, d_model)
    g_final = norm_final_g.reshape(1, d_model)
    g_sub = subln_g[layer].reshape(1, HEAD_V)
    scale_pool = pool_scale[layer].reshape(1, d_pool)
    lam_vecs = [v[layer].reshape(1, HEAD_QK) for v in (lambda_q1, lambda_k1, lambda_q2, lambda_k2)]
    w_in_b = w_in[layer].astype(BF16)
    w_out_b = w_out[layer].astype(BF16)
    w_up_b = w_up[layer].astype(BF16)
    w_down_b = w_down[layer].astype(BF16)
    w_pool_b = w_pool[layer].astype(BF16)

    z_b = _in_proj(x_big, g_mix, w_in_b)
    z_s = _in_proj(x_small, g_mix, w_in_b)

    a_b = _prompt_attention(z_b, z_s, n_small, slopes, lam_vecs, g_sub, lam_init, n_meta)
    a_meta = _meta_attention(z_s, n_small, slopes, lam_vecs, g_sub, lam_init, n_meta)
    a_samp = _sample_attention(z_s, cache_k[layer], cache_v[layer], page_table, slopes, lam_vecs,
                               g_sub, lam_init, ld)
    a_s = jnp.concatenate([a_samp, a_meta], axis=0)

    u_b = z_b[:, 3 * D_ATTN:]
    u_s = z_s[:n_small, 3 * D_ATTN:].reshape(bd, ld, d_pool)
    u_meta = z_s[n_small:, 3 * D_ATTN:]
    hist = state_pool[layer].astype(F32)
    n_hist = hist.shape[1]
    zpad = jnp.zeros((bd, POOL_HALO - n_hist, d_pool), F32)
    ext = jnp.concatenate([jnp.concatenate([zpad, hist, u_s], axis=1).reshape(-1, d_pool),
                           jnp.zeros((POOL_HALO, d_pool), F32), u_meta], axis=0)
    m_b = _pool_big(z_b, z_s, n_small, w_pool_b, scale_pool)
    m_s = _pool_small(ext, w_pool_b, scale_pool, bd, ld, n_meta)

    h_b = _out_proj(x_big, a_b, m_b, w_out_b)
    h_s = _out_proj(x_small, a_s, m_s, w_out_b)
    y_b = _mlp(h_b, g_mlp, w_up_b, w_down_b, g_final)
    y_s = _mlp(h_s, g_mlp, w_up_b, w_down_b, g_final)

    def heads(z, col):
        return z[:, col * D_ATTN:(col + 1) * D_ATTN].reshape(-1, N_HEADS, HEAD_V)

    t = seq + n_meta
    k_prompt = jnp.concatenate([heads(z_s[n_small:], 1), heads(z_b, 1)], axis=0)
    v_prompt = jnp.concatenate([heads(z_s[n_small:], 2), heads(z_b, 2)], axis=0)
    u_ext = jnp.concatenate([hist, u_s], axis=1)
    return (y_b.reshape(1, seq, d_model),
            y_s[:n_small].reshape(bd, ld, d_model),
            k_prompt.reshape(1, 1, t, N_HEADS, HEAD_V),
            v_prompt.reshape(1, 1, t, N_HEADS, HEAD_V),
            u_b[-n_hist:].reshape(1, 1, n_hist, d_pool),
            heads(z_s[:n_small], 1).reshape(1, bd, ld, N_HEADS, HEAD_V),
            heads(z_s[:n_small], 2).reshape(1, bd, ld, N_HEADS, HEAD_V),
            u_ext[:, -n_hist:].reshape(1, bd, n_hist, d_pool))
```

```python
import functools
import math

import jax
import jax.numpy as jnp
from jax import lax
from jax.experimental import pallas as pl
from jax.experimental.pallas import tpu as pltpu

N_HEADS = 8
HEAD_V = 128
HEAD_QK = HEAD_V // 2
D_ATTN = N_HEADS * HEAD_V
POOL_WINDOWS = (2, 4, 8, 16)
POOL_HALO = 16
EPS = 1e-6
MASK_VALUE = -1e30
QK_SCALE = HEAD_QK ** -0.5
LOG2E = 1.4426950408889634
POS_RADIX = 32
ACC_ROWS = HEAD_V + 16

VMEM_LIMIT_BYTES = 56 * 1024 * 1024
MAX_PAGES_PER_STEP = 16
BF16 = jnp.bfloat16
F32 = jnp.float32

_NT = (((1,), (1,)), ((), ()))


def _params(n_grid_dims):
    return pltpu.CompilerParams(dimension_semantics=("arbitrary",) * n_grid_dims,
                                vmem_limit_bytes=VMEM_LIMIT_BYTES)


def _row_tile(rows, target):
    best = rows
    for t in range(16, min(rows, target) + 1, 16):
        if rows % t == 0:
            best = t
    return best if best <= target else rows


def _rmsnorm(x, g):
    ms = jnp.mean(x * x, axis=-1, keepdims=True)
    return x * lax.rsqrt(ms + EPS) * g


def _in_proj_kernel(x_ref, g_ref, w_ref, z_ref, xn_ref):
    @pl.when(pl.program_id(1) == 0)
    def _():
        xn_ref[...] = _rmsnorm(x_ref[...], g_ref[...]).astype(BF16)

    z_ref[...] = jnp.dot(xn_ref[...], w_ref[...], preferred_element_type=F32)


def _in_proj(x, g, w, tm_target=512, tn=2048):
    rows, d = x.shape
    n = w.shape[1]
    tm = _row_tile(rows, tm_target)
    return pl.pallas_call(
        _in_proj_kernel,
        grid=(rows // tm, n // tn),
        in_specs=[pl.BlockSpec((tm, d), lambda i, j: (i, 0)),
                  pl.BlockSpec((1, d), lambda i, j: (0, 0)),
                  pl.BlockSpec((d, tn), lambda i, j: (0, j))],
        out_specs=pl.BlockSpec((tm, tn), lambda i, j: (i, j)),
        out_shape=jax.ShapeDtypeStruct((rows, n), F32),
        scratch_shapes=[pltpu.VMEM((tm, d), BF16)],
        compiler_params=_params(2),
        name="in_proj",
    )(x, g, w)


def _stack_maps(q, scale=QK_SCALE):
    q = q * scale
    lane = lax.broadcasted_iota(jnp.int32, q.shape, 1)
    q1 = jnp.where(lane < HEAD_QK, q, 0.0)
    q2 = jnp.where(lane >= HEAD_QK, q, 0.0)
    return jnp.concatenate([q1, q2], axis=0).astype(BF16)


def _diff_lambda(lq1_ref, lk1_ref, lq2_ref, lk2_ref, lam_init):
    a = jnp.sum(lq1_ref[...] * lk1_ref[...], axis=-1, keepdims=True)
    b = jnp.sum(lq2_ref[...] * lk2_ref[...], axis=-1, keepdims=True)
    return jnp.exp(a) - jnp.exp(b) + lam_init


def _diff_combine(acc, l, lam, g, lam_init):
    n = acc.shape[0] // 2
    o = acc[:n] / l[:n] - lam * (acc[n:] / l[n:])
    return _rmsnorm(o, g) * (1.0 - lam_init)


def _prompt_attn_kernel(slope_ref, q_ref, k_ref, v_ref, km_ref, vm_ref,
                        lq1_ref, lk1_ref, lq2_ref, lk2_ref, gcol_ref, o_ref,
                        kb_ref, vt_ref, kmb_ref, vmt_ref, q2_ref, m_ref, acc_ref,
                        mask_ref, sa_ref, sb_ref, *, tq, n_meta, lam_init):
    h = pl.program_id(0)
    i = pl.program_id(1)
    slope = slope_ref[h] * LOG2E
    n_chunks = kb_ref.shape[0]

    @pl.when(i == 0)
    def _():
        key = lax.broadcasted_iota(jnp.int32, (tq, HEAD_V), 0)
        lane = lax.broadcasted_iota(jnp.int32, (tq, HEAD_V), 1)
        kfeat = jnp.where(lane < 3, lax.div(key, POS_RADIX),
                          jnp.where(lane < 6, lax.rem(key, POS_RADIX),
                                    jnp.where(lane < 9, 1, 0))).astype(BF16)
        sub = lax.broadcasted_iota(jnp.int32, (ACC_ROWS - HEAD_V, tq), 0)
        ones_row = jnp.where(sub == 0, 1.0, 0.0).astype(BF16)
        for c in range(n_chunks):
            kb_ref[c, :, :HEAD_V] = k_ref[c * tq:(c + 1) * tq, :].astype(BF16)
            kb_ref[c, :, HEAD_V:] = kfeat
            vt_ref[c, :HEAD_V, :] = v_ref[c * tq:(c + 1) * tq, :].T.astype(BF16)
            vt_ref[c, HEAD_V:, :] = ones_row
        pad = jnp.zeros((HEAD_V - n_meta, HEAD_V), F32)
        kmb_ref[:, :HEAD_V] = jnp.concatenate([km_ref[...], pad], axis=0).astype(BF16)
        kmb_ref[:, HEAD_V:] = jnp.zeros((HEAD_V, HEAD_V), BF16)
        vmt_ref[:HEAD_V, :] = jnp.concatenate([vm_ref[...], pad], axis=0).T.astype(BF16)
        vmt_ref[HEAD_V:, :] = ones_row[:, :HEAD_V]
        key = lax.broadcasted_iota(jnp.int32, (tq, 2 * tq), 0)
        qry = lax.broadcasted_iota(jnp.int32, (tq, 2 * tq), 1)
        qry = jnp.where(qry >= tq, qry - tq, qry)
        mask_ref[...] = jnp.where(key <= qry, 0.0, MASK_VALUE)

    q2_ref[:, :HEAD_V] = _stack_maps(q_ref[...], QK_SCALE * LOG2E)
    qrow = lax.broadcasted_iota(jnp.int32, (2 * tq, HEAD_V), 0)
    lane = lax.broadcasted_iota(jnp.int32, (2 * tq, HEAD_V), 1)
    qrow = jnp.where(qrow >= tq, qrow - tq, qrow).astype(F32)
    whole = jnp.where(lane < 3, slope * POS_RADIX, jnp.where(lane < 6, slope, -slope * qrow))
    piece1 = whole.astype(BF16).astype(F32)
    rest = whole - piece1
    piece2 = rest.astype(BF16).astype(F32)
    piece3 = rest - piece2
    third = lax.rem(lane, 3)
    qfeat = jnp.where(third == 0, piece1, jnp.where(third == 1, piece2, piece3))
    q2_ref[:, HEAD_V:] = jnp.where(lane < 9, qfeat, 0.0).astype(BF16)

    def scores(c, s_ref):
        s_ref[...] = lax.dot_general(kb_ref[c], q2_ref[...], _NT, preferred_element_type=F32)

    scores(0, sa_ref)

    key = lax.broadcasted_iota(jnp.int32, (HEAD_V, 2 * tq), 0)
    qry = lax.broadcasted_iota(jnp.int32, (HEAD_V, 2 * tq), 1)
    qry = jnp.where(qry >= tq, qry - tq, qry)
    qpos = n_meta + i * tq + qry
    s = lax.dot_general(kmb_ref[...], q2_ref[...], _NT, preferred_element_type=F32)
    s = jnp.where(key < n_meta, s + slope * (key - qpos).astype(F32), MASK_VALUE)
    m0 = jnp.max(s, axis=0, keepdims=True)
    m_ref[...] = m0
    acc_ref[...] = jnp.dot(vmt_ref[...], jnp.exp2(s - m0).astype(BF16), preferred_element_type=F32)

    def softmax_update(c, s_ref, diagonal):
        s = s_ref[...]
        if diagonal:
            s = s + mask_ref[...]
        sigma = slope * (tq * (c - i)).astype(F32)
        m_old = m_ref[...]
        m_new = jnp.maximum(m_old, jnp.max(s, axis=0, keepdims=True) + sigma)
        p = jnp.exp2(s - (m_new - sigma)).astype(BF16)
        alpha = jnp.exp2(m_old - m_new)
        acc_ref[...] = alpha * acc_ref[...] + jnp.dot(vt_ref[c], p, preferred_element_type=F32)
        m_ref[...] = m_new

    def pair(j, carry):
        c = 2 * j
        scores(c + 1, sb_ref)
        softmax_update(c, sa_ref, False)
        scores(c + 2, sa_ref)
        softmax_update(c + 1, sb_ref, False)
        return carry

    lax.fori_loop(0, lax.shift_right_logical(i, 1), pair, 0)

    @pl.when((i & 1) == 0)
    def _():
        softmax_update(i, sa_ref, True)

    @pl.when((i & 1) == 1)
    def _():
        scores(i, sb_ref)
        softmax_update(i - 1, sa_ref, False)
        softmax_update(i, sb_ref, True)

    lam = _diff_lambda(lq1_ref, lk1_ref, lq2_ref, lk2_ref, lam_init)
    acc = acc_ref[:HEAD_V, :]
    l = acc_ref[HEAD_V:HEAD_V + 1, :]
    o = acc[:, :tq] / l[:, :tq] - lam * (acc[:, tq:] / l[:, tq:])
    ms = jnp.mean(o * o, axis=0, keepdims=True)
    o = o * lax.rsqrt(ms + EPS) * gcol_ref[...] * (1.0 - lam_init)
    o_ref[...] = o.T.astype(o_ref.dtype)


def _prompt_attention(z_b, z_s, meta_row0, slopes, lam_vecs, subln_g, lam_init, n_meta, tq=512):
    seq = z_b.shape[0]
    tq = _row_tile(seq, tq)
    assert tq % 128 == 0 and meta_row0 % n_meta == 0 and n_meta <= HEAD_V
    assert tq <= POS_RADIX * POS_RADIX, "key index digits must be exact in bf16"
    meta_blk = meta_row0 // n_meta
    vec = pl.BlockSpec((1, HEAD_QK), lambda h, i: (0, 0))
    kern = functools.partial(_prompt_attn_kernel, tq=tq, n_meta=n_meta, lam_init=lam_init)
    return pl.pallas_call(
        kern,
        grid=(N_HEADS, seq // tq),
        in_specs=[pl.BlockSpec(memory_space=pltpu.SMEM),
                  pl.BlockSpec((tq, HEAD_V), lambda h, i: (i, h)),
                  pl.BlockSpec((seq, HEAD_V), lambda h, i: (0, N_HEADS + h)),
                  pl.BlockSpec((seq, HEAD_V), lambda h, i: (0, 2 * N_HEADS + h)),
                  pl.BlockSpec((n_meta, HEAD_V), lambda h, i: (meta_blk, N_HEADS + h)),
                  pl.BlockSpec((n_meta, HEAD_V), lambda h, i: (meta_blk, 2 * N_HEADS + h)),
                  vec, vec, vec, vec,
                  pl.BlockSpec((HEAD_V, 1), lambda h, i: (0, 0))],
        out_specs=pl.BlockSpec((tq, HEAD_V), lambda h, i: (i, h)),
        out_shape=jax.ShapeDtypeStruct((seq, D_ATTN), BF16),
        scratch_shapes=[pltpu.VMEM((seq // tq, tq, 2 * HEAD_V), BF16),
                        pltpu.VMEM((seq // tq, ACC_ROWS, tq), BF16),
                        pltpu.VMEM((HEAD_V, 2 * HEAD_V), BF16),
                        pltpu.VMEM((ACC_ROWS, HEAD_V), BF16),
                        pltpu.VMEM((2 * tq, 2 * HEAD_V), BF16),
                        pltpu.VMEM((1, 2 * tq), F32),
                        pltpu.VMEM((ACC_ROWS, 2 * tq), F32),
                        pltpu.VMEM((tq, 2 * tq), F32),
                        pltpu.VMEM((tq, 2 * tq), F32),
                        pltpu.VMEM((tq, 2 * tq), F32)],
        compiler_params=_params(2),
        name="prompt_attn",
    )(slopes, z_b, z_b, z_b, z_s, z_s, *lam_vecs, subln_g.reshape(HEAD_V, 1))


def _meta_attn_kernel(slope_ref, q_ref, k_ref, v_ref, lq1_ref, lk1_ref, lq2_ref, lk2_ref, g_ref,
                      o_ref, *, n_meta, lam_init):
    slope = slope_ref[pl.program_id(0)]
    q2 = _stack_maps(q_ref[...])
    row = lax.broadcasted_iota(jnp.int32, (2 * n_meta, n_meta), 0)
    col = lax.broadcasted_iota(jnp.int32, (2 * n_meta, n_meta), 1)
    row = jnp.where(row >= n_meta, row - n_meta, row)
    s = lax.dot_general(q2, k_ref[...].astype(BF16), _NT, preferred_element_type=F32)
    s = jnp.where(col <= row, s + slope * (col - row).astype(F32), MASK_VALUE)
    p = jnp.exp(s - jnp.max(s, axis=-1, keepdims=True))
    l = jnp.sum(p, axis=-1, keepdims=True)
    acc = jnp.dot(p.astype(BF16), v_ref[...].astype(BF16), preferred_element_type=F32)
    lam = _diff_lambda(lq1_ref, lk1_ref, lq2_ref, lk2_ref, lam_init)
    o_ref[...] = _diff_combine(acc, l, lam, g_ref[...], lam_init)


def _meta_attention(z_s, meta_row0, slopes, lam_vecs, subln_g, lam_init, n_meta):
    meta_blk = meta_row0 // n_meta
    vec = pl.BlockSpec((1, HEAD_QK), lambda h: (0, 0))
    kern = functools.partial(_meta_attn_kernel, n_meta=n_meta, lam_init=lam_init)
    return pl.pallas_call(
        kern,
        grid=(N_HEADS,),
        in_specs=[pl.BlockSpec(memory_space=pltpu.SMEM),
                  pl.BlockSpec((n_meta, HEAD_V), lambda h: (meta_blk, h)),
                  pl.BlockSpec((n_meta, HEAD_V), lambda h: (meta_blk, N_HEADS + h)),
                  pl.BlockSpec((n_meta, HEAD_V), lambda h: (meta_blk, 2 * N_HEADS + h)),
                  vec, vec, vec, vec,
                  pl.BlockSpec((1, HEAD_V), lambda h: (0, 0))],
        out_specs=pl.BlockSpec((n_meta, HEAD_V), lambda h: (0, h)),
        out_shape=jax.ShapeDtypeStruct((n_meta, D_ATTN), F32),
        compiler_params=_params(1),
        name="meta_attn",
    )(slopes, z_s, z_s, z_s, *lam_vecs, subln_g)


def _sample_attn_kernel(pt_ref, slope_ref, q_ref, kn_ref, vn_ref, *rest,
                        pages_per_step, page_size, ld, past, lam_init):
    del pt_ref
    P = pages_per_step
    k_pages = rest[:P]
    v_pages = rest[P:2 * P]
    (lq1_ref, lk1_ref, lq2_ref, lk2_ref, g_ref, o_ref,
     q2_ref, m_ref, l_ref, acc_ref, bias_ref) = rest[2 * P:]
    c = pl.program_id(1)
    tk = P * page_size
    n_pairs = N_HEADS // 2
    hr = 2 * ld
    pr = 2 * hr
    rows = N_HEADS * hr
    pair_heads = [(j, j + n_pairs) for j in range(n_pairs)]

    slope_col = jnp.concatenate([jnp.full((hr, 1), slope_ref[h], F32)
                                 for pair in pair_heads for h in pair], axis=0)

    def row_head_and_query(shape):
        row = lax.broadcasted_iota(jnp.int32, shape, 0)
        return lax.rem(lax.div(row, hr), 2), lax.rem(row, ld)

    @pl.when((pl.program_id(0) == 0) & (c == 0))
    def _():
        rh, qi = row_head_and_query((rows, 2 * tk))
        col = lax.broadcasted_iota(jnp.int32, (rows, 2 * tk), 1)
        rel = (lax.div(col, 2) - qi).astype(F32)
        bias_ref[...] = jnp.where(lax.rem(col, 2) == rh, slope_col * rel, MASK_VALUE)

    @pl.when(c == 0)
    def _():
        for j, pair in enumerate(pair_heads):
            q2_ref[j] = jnp.concatenate(
                [_stack_maps(q_ref[:, h * HEAD_V:(h + 1) * HEAD_V]) for h in pair], axis=0)
        m_ref[...] = jnp.full(m_ref.shape, MASK_VALUE, F32)
        l_ref[...] = jnp.zeros(l_ref.shape, F32)
        acc_ref[...] = jnp.zeros(acc_ref.shape, F32)

    def update(s, values):
        m_old = m_ref[...]
        m_new = jnp.maximum(m_old, jnp.max(s, axis=-1, keepdims=True))
        p = jnp.exp(s - m_new)
        alpha = jnp.exp(m_old - m_new)
        l_ref[...] = alpha * l_ref[...] + jnp.sum(p, axis=-1, keepdims=True)
        p = p.astype(BF16)
        pv = [jnp.dot(p[j * pr:(j + 1) * pr], values[j], preferred_element_type=F32)
              for j in range(n_pairs)]
        acc_ref[...] = alpha * acc_ref[...] + jnp.concatenate(pv, axis=0)
        m_ref[...] = m_new

    def pair_rows(pages, j):
        return jnp.concatenate([pg[0, pl.ds(j, 2 * page_size, stride=n_pairs), :] for pg in pages],
                               axis=0).astype(BF16)

    s_parts, values = [], []
    for j in range(n_pairs):
        s_parts.append(lax.dot_general(q2_ref[j], pair_rows(k_pages, j), _NT,
                                       preferred_element_type=F32))
        values.append(pair_rows(v_pages, j))
    group_offset = slope_col * (c * tk - past).astype(F32)
    update(jnp.concatenate(s_parts, axis=0) + bias_ref[...] + group_offset, values)

    @pl.when(c == pl.num_programs(1) - 1)
    def _():
        lam = _diff_lambda(lq1_ref, lk1_ref, lq2_ref, lk2_ref, lam_init)
        rh, qi = row_head_and_query((rows, hr))
        col = lax.broadcasted_iota(jnp.int32, (rows, hr), 1)
        kj = lax.rem(col, ld)
        visible = (lax.div(col, ld) == rh) & (kj <= qi)
        s_parts, values = [], []
        for j, pair in enumerate(pair_heads):
            kn = jnp.concatenate([kn_ref[:, h * HEAD_V:(h + 1) * HEAD_V] for h in pair], axis=0)
            vn = jnp.concatenate([vn_ref[:, h * HEAD_V:(h + 1) * HEAD_V] for h in pair], axis=0)
            s_parts.append(lax.dot_general(q2_ref[j], kn.astype(BF16), _NT,
                                           preferred_element_type=F32))
            values.append(vn.astype(BF16))
        s = jnp.concatenate(s_parts, axis=0) + slope_col * (kj - qi).astype(F32)
        update(jnp.where(visible, s, MASK_VALUE), values)
        acc = acc_ref[...]
        l = l_ref[...]
        for j, pair in enumerate(pair_heads):
            for t, h in enumerate(pair):
                r0 = j * pr + t * hr
                o_ref[:, h * HEAD_V:(h + 1) * HEAD_V] = _diff_combine(
                    acc[r0:r0 + hr], l[r0:r0 + hr], lam, g_ref[...], lam_init)


def _sample_attention(z_s, cache_k, cache_v, page_table, slopes, lam_vecs, subln_g, lam_init, ld):
    bd, n_pages = page_table.shape
    n_phys, page_size = cache_k.shape[0], cache_k.shape[1]
    past = n_pages * page_size
    P = max(p for p in range(1, MAX_PAGES_PER_STEP + 1) if n_pages % p == 0)
    k2 = cache_k.reshape(n_phys, page_size * N_HEADS, HEAD_V)
    v2 = cache_v.reshape(n_phys, page_size * N_HEADS, HEAD_V)
    vec = pl.BlockSpec((1, HEAD_QK), lambda b, c, pt: (0, 0))

    def page_spec(j):
        return pl.BlockSpec((1, page_size * N_HEADS, HEAD_V),
                            lambda b, c, pt: (pt[b * n_pages + c * P + j], 0, 0))

    kern = functools.partial(_sample_attn_kernel, pages_per_step=P, page_size=page_size, ld=ld,
                             past=past, lam_init=lam_init)
    grid_spec = pltpu.PrefetchScalarGridSpec(
        num_scalar_prefetch=1,
        grid=(bd, n_pages // P),
        in_specs=[pl.BlockSpec(memory_space=pltpu.SMEM),
                  pl.BlockSpec((ld, D_ATTN), lambda b, c, pt: (b, 0)),
                  pl.BlockSpec((ld, D_ATTN), lambda b, c, pt: (b, 1)),
                  pl.BlockSpec((ld, D_ATTN), lambda b, c, pt: (b, 2))]
                 + [page_spec(j) for j in range(P)] + [page_spec(j) for j in range(P)]
                 + [vec, vec, vec, vec, pl.BlockSpec((1, HEAD_V), lambda b, c, pt: (0, 0))],
        out_specs=pl.BlockSpec((ld, D_ATTN), lambda b, c, pt: (b, 0)),
        scratch_shapes=[pltpu.VMEM((N_HEADS // 2, 4 * ld, HEAD_V), BF16),
                        pltpu.VMEM((N_HEADS * 2 * ld, 1), F32),
                        pltpu.VMEM((N_HEADS * 2 * ld, 1), F32),
                        pltpu.VMEM((N_HEADS * 2 * ld, HEAD_V), F32),
                        pltpu.VMEM((N_HEADS * 2 * ld, 2 * P * page_size), F32)])
    return pl.pallas_call(
        kern,
        grid_spec=grid_spec,
        out_shape=jax.ShapeDtypeStruct((bd * ld, D_ATTN), F32),
        compiler_params=_params(2),
        name="sample_attn",
    )(page_table.reshape(-1), slopes, z_s, z_s, z_s, *([k2] * P), *([v2] * P), *lam_vecs, subln_g)


def _window_sums(ext):
    out = {}
    s = ext
    w = 1
    while w < max(POOL_WINDOWS):
        s = s + pltpu.roll(s, w, 0)
        w *= 2
        out[w] = s
    return out


def _pool_project(u, sums, inv_cnt, wp_ref, scale_ref):
    cg = u.shape[1] // len(POOL_WINDOWS)
    outs = []
    for g, w in enumerate(POOL_WINDOWS):
        sl = slice(g * cg, (g + 1) * cg)
        d = sums[w][:, sl] * inv_cnt[w] - u[:, sl]
        outs.append(jnp.dot(d.astype(BF16), wp_ref[g], preferred_element_type=F32))
    return jnp.concatenate(outs, axis=1) * scale_ref[...]


def _pool_big_kernel(u_ref, prev_ref, meta_ref, wp_ref, scale_ref, m_ref):
    halo = jnp.where(pl.program_id(0) == 0, meta_ref[...], prev_ref[...])
    u = u_ref[...]
    sums = _window_sums(jnp.concatenate([halo, u], axis=0))
    sums = {w: s[POOL_HALO:] for w, s in sums.items()}
    inv = {w: 1.0 / w for w in POOL_WINDOWS}
    m_ref[...] = _pool_project(u, sums, inv, wp_ref, scale_ref).astype(m_ref.dtype)


def _pool_big(z_b, z_s, meta_row0, w_pool, pool_scale, tm=512):
    seq = z_b.shape[0]
    d_pool = w_pool.shape[0] * w_pool.shape[1]
    tm = _row_tile(seq, tm)
    ucol = z_b.shape[1] // d_pool - 1
    per = tm // POOL_HALO
    return pl.pallas_call(
        _pool_big_kernel,
        grid=(seq // tm,),
        in_specs=[pl.BlockSpec((tm, d_pool), lambda i: (i, ucol)),
                  pl.BlockSpec((POOL_HALO, d_pool), lambda i: (jnp.maximum(i * per - 1, 0), ucol)),
                  pl.BlockSpec((POOL_HALO, d_pool), lambda i: (meta_row0 // POOL_HALO, ucol)),
                  pl.BlockSpec(w_pool.shape, lambda i: (0, 0, 0)),
                  pl.BlockSpec((1, d_pool), lambda i: (0, 0))],
        out_specs=pl.BlockSpec((tm, d_pool), lambda i: (i, 0)),
        out_shape=jax.ShapeDtypeStruct((seq, d_pool), BF16),
        compiler_params=_params(1),
        name="pool_big",
    )(z_b, z_b, z_s, w_pool, pool_scale)


def _pool_small_kernel(ext_ref, wp_ref, scale_ref, m_ref, *, bd, ld, n_meta):
    grp = POOL_HALO + ld
    ext = ext_ref[...]
    sums = _window_sums(ext)
    meta0 = bd * grp + POOL_HALO

    def new_rows(a):
        parts = [a[b * grp + POOL_HALO:(b + 1) * grp] for b in range(bd)]
        return jnp.concatenate(parts + [a[meta0:meta0 + n_meta]], axis=0)

    n = bd * ld + n_meta
    r = lax.broadcasted_iota(jnp.int32, (n, 1), 0)
    inv = {}
    for w in POOL_WINDOWS:
        cnt = jnp.where(r < bd * ld, w, jnp.minimum(w, r - bd * ld + 1))
        inv[w] = 1.0 / cnt.astype(F32)
    sums = {w: new_rows(s) for w, s in sums.items()}
    m_ref[...] = _pool_project(new_rows(ext), sums, inv, wp_ref, scale_ref).astype(m_ref.dtype)


def _pool_small(ext, w_pool, pool_scale, bd, ld, n_meta):
    n = bd * ld + n_meta
    d_pool = ext.shape[1]
    kern = functools.partial(_pool_small_kernel, bd=bd, ld=ld, n_meta=n_meta)
    return pl.pallas_call(
        kern,
        grid=(1,),
        in_specs=[pl.BlockSpec(ext.shape, lambda i: (0, 0)),
                  pl.BlockSpec(w_pool.shape, lambda i: (0, 0, 0)),
                  pl.BlockSpec((1, d_pool), lambda i: (0, 0))],
        out_specs=pl.BlockSpec((n, d_pool), lambda i: (0, 0)),
        out_shape=jax.ShapeDtypeStruct((n, d_pool), BF16),
        compiler_params=_params(1),
        name="pool_small",
    )(ext, w_pool, pool_scale)


def _out_proj_kernel(x_ref, a_ref, m_ref, wa_ref, wm_ref, h_ref):
    h_ref[...] = (x_ref[...]
                  + jnp.dot(a_ref[...].astype(BF16), wa_ref[...], preferred_element_type=F32)
                  + jnp.dot(m_ref[...], wm_ref[...], preferred_element_type=F32))


def _out_proj(x, a, m, w_out, tm=512):
    rows, d = x.shape
    da, dm = a.shape[1], m.shape[1]
    tm = _row_tile(rows, tm)
    return pl.pallas_call(
        _out_proj_kernel,
        grid=(rows // tm,),
        in_specs=[pl.BlockSpec((tm, d), lambda i: (i, 0)),
                  pl.BlockSpec((tm, da), lambda i: (i, 0)),
                  pl.BlockSpec((tm, dm), lambda i: (i, 0)),
                  pl.BlockSpec((da, d), lambda i: (0, 0)),
                  pl.BlockSpec((dm, d), lambda i: (da // dm, 0))],
        out_specs=pl.BlockSpec((tm, d), lambda i: (i, 0)),
        out_shape=jax.ShapeDtypeStruct((rows, d), F32),
        compiler_params=_params(1),
        name="out_proj",
    )(x, a, m, w_out, w_out)


def _mlp_kernel(h_ref, g_ref, wu_ref, wd_ref, gf_ref, y_ref, xn_ref):
    j = pl.program_id(1)

    @pl.when(j == 0)
    def _():
        h = h_ref[...]
        xn_ref[...] = _rmsnorm(h, g_ref[...]).astype(BF16)
        y_ref[...] = h

    a = jnp.maximum(jnp.dot(xn_ref[...], wu_ref[...], preferred_element_type=F32), 0.0)
    y_ref[...] += jnp.dot((a * a).astype(BF16), wd_ref[...], preferred_element_type=F32)

    @pl.when(j == pl.num_programs(1) - 1)
    def _():
        y_ref[...] = _rmsnorm(y_ref[...], gf_ref[...])


def _mlp(h, g, w_up, w_down, g_final, tm=512, tf=1024):
    rows, d = h.shape
    d_ff = w_up.shape[1]
    tm = _row_tile(rows, tm)
    return pl.pallas_call(
        _mlp_kernel,
        grid=(rows // tm, d_ff // tf),
        in_specs=[pl.BlockSpec((tm, d), lambda i, j: (i, 0)),
                  pl.BlockSpec((1, d), lambda i, j: (0, 0)),
                  pl.BlockSpec((d, tf), lambda i, j: (0, j)),
                  pl.BlockSpec((tf, d), lambda i, j: (j, 0)),
                  pl.BlockSpec((1, d), lambda i, j: (0, 0))],
        out_specs=pl.BlockSpec((tm, d), lambda i, j: (i, 0)),
        out_shape=jax.ShapeDtypeStruct((rows, d), F32),
        scratch_shapes=[pltpu.VMEM((tm, d), BF16)],
        compiler_params=_params(2),
        name="mlp",
    )(h, g, w_up, w_down, g_final)


def kernel(x_prompt, x_sample, cache_k, cache_v, state_pool, page_table, meta_tokens, norm_mix_g,
           w_in, lambda_q1, lambda_k1, lambda_q2, lambda_k2, subln_g, w_pool, pool_scale, w_out,
           norm_mlp_g, w_up, w_down, norm_final_g):
    depth = norm_mix_g.shape[0]
    batch, seq, d_model = x_prompt.shape
    bd, ld, _ = x_sample.shape
    n_meta = meta_tokens.shape[0]
    assert depth == 1 and batch == 1, "one layer and one prompt sequence are supported"
    assert n_meta == POOL_HALO and (bd * ld) % n_meta == 0 and seq >= POOL_HALO
    layer = 0
    lam_init = 0.8 - 0.6 * math.exp(-0.3 * layer)
    slopes = 2.0 ** (-8.0 * jnp.arange(1, N_HEADS + 1, dtype=F32) / N_HEADS)
    n_small = bd * ld
    d_pool = w_pool.shape[1] * w_pool.shape[2]

    x_big = x_prompt.reshape(seq, d_model)
    x_small = jnp.concatenate([x_sample.reshape(n_small, d_model), meta_tokens.astype(F32)], axis=0)

    g_mix = norm_mix_g[layer].reshape(1, d_model)
    g_mlp = norm_mlp_g[layer].reshape(1, d_model)
    g_final = norm_final_g.reshape(1, d_model)
    g_sub = subln_g[layer].reshape(1, HEAD_V)
    scale_pool = pool_scale[layer].reshape(1, d_pool)
    lam_vecs = [v[layer].reshape(1, HEAD_QK) for v in (lambda_q1, lambda_k1, lambda_q2, lambda_k2)]
    w_in_b = w_in[layer].astype(BF16)
    w_out_b = w_out[layer].astype(BF16)
    w_up_b = w_up[layer].astype(BF16)
    w_down_b = w_down[layer].astype(BF16)
    w_pool_b = w_pool[layer].astype(BF16)

    z_b = _in_proj(x_big, g_mix, w_in_b)
    z_s = _in_proj(x_small, g_mix, w_in_b)

    a_b = _prompt_attention(z_b, z_s, n_small, slopes, lam_vecs, g_sub, lam_init, n_meta)
    a_meta = _meta_attention(z_s, n_small, slopes, lam_vecs, g_sub, lam_init, n_meta)
    a_samp = _sample_attention(z_s, cache_k[layer], cache_v[layer], page_table, slopes, lam_vecs,
                               g_sub, lam_init, ld)
    a_s = jnp.concatenate([a_samp, a_meta], axis=0)

    u_b = z_b[:, 3 * D_ATTN:]
    u_s = z_s[:n_small, 3 * D_ATTN:].reshape(bd, ld, d_pool)
    u_meta = z_s[n_small:, 3 * D_ATTN:]
    hist = state_pool[layer].astype(F32)
    n_hist = hist.shape[1]
    zpad = jnp.zeros((bd, POOL_HALO - n_hist, d_pool), F32)
    ext = jnp.concatenate([jnp.concatenate([zpad, hist, u_s], axis=1).reshape(-1, d_pool),
                           jnp.zeros((POOL_HALO, d_pool), F32), u_meta], axis=0)
    m_b = _pool_big(z_b, z_s, n_small, w_pool_b, scale_pool)
    m_s = _pool_small(ext, w_pool_b, scale_pool, bd, ld, n_meta)

    h_b = _out_proj(x_big, a_b, m_b, w_out_b)
    h_s = _out_proj(x_small, a_s, m_s, w_out_b)
    y_b = _mlp(h_b, g_mlp, w_up_b, w_down_b, g_final)
    y_s = _mlp(h_s, g_mlp, w_up_b, w_down_b, g_final)

    def heads(z, col):
        return z[:, col * D_ATTN:(col + 1) * D_ATTN].reshape(-1, N_HEADS, HEAD_V)

    t = seq + n_meta
    k_prompt = jnp.concatenate([heads(z_s[n_small:], 1), heads(z_b, 1)], axis=0)
    v_prompt = jnp.concatenate([heads(z_s[n_small:], 2), heads(z_b, 2)], axis=0)
    u_ext = jnp.concatenate([hist, u_s], axis=1)
    return (y_b.reshape(1, seq, d_model),
            y_s[:n_small].reshape(bd, ld, d_model),
            k_prompt.reshape(1, 1, t, N_HEADS, HEAD_V),
            v_prompt.reshape(1, 1, t, N_HEADS, HEAD_V),
            u_b[-n_hist:].reshape(1, 1, n_hist, d_pool),
            heads(z_s[:n_small], 1).reshape(1, bd, ld, N_HEADS, HEAD_V),
            heads(z_s[:n_small], 2).reshape(1, bd, ld, N_HEADS, HEAD_V),
            u_ext[:, -n_hist:].reshape(1, bd, n_hist, d_pool))
```

```python
import functools
import math

import jax
import jax.numpy as jnp
from jax import lax
from jax.experimental import pallas as pl
from jax.experimental.pallas import tpu as pltpu

N_HEADS = 8
HEAD_V = 128
HEAD_QK = HEAD_V // 2
D_ATTN = N_HEADS * HEAD_V
POOL_WINDOWS = (2, 4, 8, 16)
POOL_HALO = 16
EPS = 1e-6
MASK_VALUE = -1e30
QK_SCALE = HEAD_QK ** -0.5
LOG2E = 1.4426950408889634
POS_RADIX = 32
ACC_ROWS = HEAD_V + 16

VMEM_LIMIT_BYTES = 56 * 1024 * 1024
MAX_PAGES_PER_STEP = 16
BF16 = jnp.bfloat16
F32 = jnp.float32

_NT = (((1,), (1,)), ((), ()))


def _params(n_grid_dims):
    return pltpu.CompilerParams(dimension_semantics=("arbitrary",) * n_grid_dims,
                                vmem_limit_bytes=VMEM_LIMIT_BYTES)


def _row_tile(rows, target):
    best = rows
    for t in range(16, min(rows, target) + 1, 16):
        if rows % t == 0:
            best = t
    return best if best <= target else rows


def _rmsnorm(x, g):
    ms = jnp.mean(x * x, axis=-1, keepdims=True)
    return x * lax.rsqrt(ms + EPS) * g


def _in_proj_kernel(x_ref, g_ref, w_ref, z_ref, xn_ref):
    @pl.when(pl.program_id(1) == 0)
    def _():
        xn_ref[...] = _rmsnorm(x_ref[...], g_ref[...]).astype(BF16)

    z_ref[...] = jnp.dot(xn_ref[...], w_ref[...], preferred_element_type=F32)


def _in_proj(x, g, w, tm_target=512, tn=2048):
    rows, d = x.shape
    n = w.shape[1]
    tm = _row_tile(rows, tm_target)
    return pl.pallas_call(
        _in_proj_kernel,
        grid=(rows // tm, n // tn),
        in_specs=[pl.BlockSpec((tm, d), lambda i, j: (i, 0)),
                  pl.BlockSpec((1, d), lambda i, j: (0, 0)),
                  pl.BlockSpec((d, tn), lambda i, j: (0, j))],
        out_specs=pl.BlockSpec((tm, tn), lambda i, j: (i, j)),
        out_shape=jax.ShapeDtypeStruct((rows, n), F32),
        scratch_shapes=[pltpu.VMEM((tm, d), BF16)],
        compiler_params=_params(2),
        name="in_proj",
    )(x, g, w)


def _stack_maps(q, scale=QK_SCALE):
    q = q * scale
    lane = lax.broadcasted_iota(jnp.int32, q.shape, 1)
    q1 = jnp.where(lane < HEAD_QK, q, 0.0)
    q2 = jnp.where(lane >= HEAD_QK, q, 0.0)
    return jnp.concatenate([q1, q2], axis=0).astype(BF16)


def _diff_lambda(lq1_ref, lk1_ref, lq2_ref, lk2_ref, lam_init):
    a = jnp.sum(lq1_ref[...] * lk1_ref[...], axis=-1, keepdims=True)
    b = jnp.sum(lq2_ref[...] * lk2_ref[...], axis=-1, keepdims=True)
    return jnp.exp(a) - jnp.exp(b) + lam_init


def _diff_combine(acc, l, lam, g, lam_init):
    n = acc.shape[0] // 2
    o = acc[:n] / l[:n] - lam * (acc[n:] / l[n:])
    return _rmsnorm(o, g) * (1.0 - lam_init)


def _prompt_attn_kernel(slope_ref, q_ref, k_ref, v_ref, km_ref, vm_ref,
                        lq1_ref, lk1_ref, lq2_ref, lk2_ref, gcol_ref, o_ref,
                        kb_ref, vt_ref, kmb_ref, vmt_ref, q2_ref, m_ref, acc_ref,
                        mask_ref, sa_ref, sb_ref, *, tq, n_meta, lam_init):
    h = pl.program_id(0)
    i = pl.program_id(1)
    slope = slope_ref[h] * LOG2E
    n_chunks = kb_ref.shape[0]

    @pl.when(i == 0)
    def _():
        key = lax.broadcasted_iota(jnp.int32, (tq, HEAD_V), 0)
        lane = lax.broadcasted_iota(jnp.int32, (tq, HEAD_V), 1)
        kfeat = jnp.where(lane < 3, lax.div(key, POS_RADIX),
                          jnp.where(lane < 6, lax.rem(key, POS_RADIX),
                                    jnp.where(lane < 9, 1, 0))).astype(BF16)
        sub = lax.broadcasted_iota(jnp.int32, (ACC_ROWS - HEAD_V, tq), 0)
        ones_row = jnp.where(sub == 0, 1.0, 0.0).astype(BF16)
        for c in range(n_chunks):
            kb_ref[c, :, :HEAD_V] = k_ref[c * tq:(c + 1) * tq, :].astype(BF16)
            kb_ref[c, :, HEAD_V:] = kfeat
            vt_ref[c, :HEAD_V, :] = v_ref[c * tq:(c + 1) * tq, :].T.astype(BF16)
            vt_ref[c, HEAD_V:, :] = ones_row
        pad = jnp.zeros((HEAD_V - n_meta, HEAD_V), F32)
        kmb_ref[:, :HEAD_V] = jnp.concatenate([km_ref[...], pad], axis=0).astype(BF16)
        kmb_ref[:, HEAD_V:] = jnp.zeros((HEAD_V, HEAD_V), BF16)
        vmt_ref[:HEAD_V, :] = jnp.concatenate([vm_ref[...], pad], axis=0).T.astype(BF16)
        vmt_ref[HEAD_V:, :] = ones_row[:, :HEAD_V]
        key = lax.broadcasted_iota(jnp.int32, (tq, 2 * tq), 0)
        qry = lax.broadcasted_iota(jnp.int32, (tq, 2 * tq), 1)
        qry = jnp.where(qry >= tq, qry - tq, qry)
        mask_ref[...] = jnp.where(key <= qry, 0.0, MASK_VALUE)

    q2_ref[:, :HEAD_V] = _stack_maps(q_ref[...], QK_SCALE * LOG2E)
    qrow = lax.broadcasted_iota(jnp.int32, (2 * tq, HEAD_V), 0)
    lane = lax.broadcasted_iota(jnp.int32, (2 * tq, HEAD_V), 1)
    qrow = jnp.where(qrow >= tq, qrow - tq, qrow).astype(F32)
    whole = jnp.where(lane < 3, slope * POS_RADIX, jnp.where(lane < 6, slope, -slope * qrow))
    piece1 = whole.astype(BF16).astype(F32)
    rest = whole - piece1
    piece2 = rest.astype(BF16).astype(F32)
    piece3 = rest - piece2
    third = lax.rem(lane, 3)
    qfeat = jnp.where(third == 0, piece1, jnp.where(third == 1, piece2, piece3))
    q2_ref[:, HEAD_V:] = jnp.where(lane < 9, qfeat, 0.0).astype(BF16)

    def scores(c, s_ref):
        s_ref[...] = lax.dot_general(kb_ref[c], q2_ref[...], _NT, preferred_element_type=F32)

    scores(0, sa_ref)

    key = lax.broadcasted_iota(jnp.int32, (HEAD_V, 2 * tq), 0)
    qry = lax.broadcasted_iota(jnp.int32, (HEAD_V, 2 * tq), 1)
    qry = jnp.where(qry >= tq, qry - tq, qry)
    qpos = n_meta + i * tq + qry
    s = lax.dot_general(kmb_ref[...], q2_ref[...], _NT, preferred_element_type=F32)
    s = jnp.where(key < n_meta, s + slope * (key - qpos).astype(F32), MASK_VALUE)
    m0 = jnp.max(s, axis=0, keepdims=True)
    m_ref[...] = m0
    acc_ref[...] = jnp.dot(vmt_ref[...], jnp.exp2(s - m0).astype(BF16), preferred_element_type=F32)

    def softmax_update(c, s_ref, diagonal):
        s = s_ref[...]
        if diagonal:
            s = s + mask_ref[...]
        sigma = slope * (tq * (c - i)).astype(F32)
        m_old = m_ref[...]
        m_new = jnp.maximum(m_old, jnp.max(s, axis=0, keepdims=True) + sigma)
        p = jnp.exp2(s - (m_new - sigma)).astype(BF16)
        alpha = jnp.exp2(m_old - m_new)
        acc_ref[...] = alpha * acc_ref[...] + jnp.dot(vt_ref[c], p, preferred_element_type=F32)
        m_ref[...] = m_new

    def pair(j, carry):
        c = 2 * j
        scores(c + 1, sb_ref)
        softmax_update(c, sa_ref, False)
        scores(c + 2, sa_ref)
        softmax_update(c + 1, sb_ref, False)
        return carry

    lax.fori_loop(0, lax.shift_right_logical(i, 1), pair, 0)

    @pl.when((i & 1) == 0)
    def _():
        softmax_update(i, sa_ref, True)

    @pl.when((i & 1) == 1)
    def _():
        scores(i, sb_ref)
        softmax_update(i - 1, sa_ref, False)
        softmax_update(i, sb_ref, True)

    lam = _diff_lambda(lq1_ref, lk1_ref, lq2_ref, lk2_ref, lam_init)
    acc = acc_ref[:HEAD_V, :]
    l = acc_ref[HEAD_V:HEAD_V + 1, :]
    o = acc[:, :tq] / l[:, :tq] - lam * (acc[:, tq:] / l[:, tq:])
    ms = jnp.mean(o * o, axis=0, keepdims=True)
    o = o * lax.rsqrt(ms + EPS) * gcol_ref[...] * (1.0 - lam_init)
    o_ref[...] = o.T.astype(o_ref.dtype)


def _prompt_attention(z_b, z_s, meta_row0, slopes, lam_vecs, subln_g, lam_init, n_meta, tq=512):
    seq = z_b.shape[0]
    tq = _row_tile(seq, tq)
    assert tq % 128 == 0 and meta_row0 % n_meta == 0 and n_meta <= HEAD_V
    assert tq <= POS_RADIX * POS_RADIX, "key index digits must be exact in bf16"
    meta_blk = meta_row0 // n_meta
    vec = pl.BlockSpec((1, HEAD_QK), lambda h, i: (0, 0))
    kern = functools.partial(_prompt_attn_kernel, tq=tq, n_meta=n_meta, lam_init=lam_init)
    return pl.pallas_call(
        kern,
        grid=(N_HEADS, seq // tq),
        in_specs=[pl.BlockSpec(memory_space=pltpu.SMEM),
                  pl.BlockSpec((tq, HEAD_V), lambda h, i: (i, h)),
                  pl.BlockSpec((seq, HEAD_V), lambda h, i: (0, N_HEADS + h)),
                  pl.BlockSpec((seq, HEAD_V), lambda h, i: (0, 2 * N_HEADS + h)),
                  pl.BlockSpec((n_meta, HEAD_V), lambda h, i: (meta_blk, N_HEADS + h)),
                  pl.BlockSpec((n_meta, HEAD_V), lambda h, i: (meta_blk, 2 * N_HEADS + h)),
                  vec, vec, vec, vec,
                  pl.BlockSpec((HEAD_V, 1), lambda h, i: (0, 0))],
        out_specs=pl.BlockSpec((tq, HEAD_V), lambda h, i: (i, h)),
        out_shape=jax.ShapeDtypeStruct((seq, D_ATTN), BF16),
        scratch_shapes=[pltpu.VMEM((seq // tq, tq, 2 * HEAD_V), BF16),
                        pltpu.VMEM((seq // tq, ACC_ROWS, tq), BF16),
                        pltpu.VMEM((HEAD_V, 2 * HEAD_V), BF16),
                        pltpu.VMEM((ACC_ROWS, HEAD_V), BF16),
                        pltpu.VMEM((2 * tq, 2 * HEAD_V), BF16),
                        pltpu.VMEM((1, 2 * tq), F32),
                        pltpu.VMEM((ACC_ROWS, 2 * tq), F32),
                        pltpu.VMEM((tq, 2 * tq), F32),
                        pltpu.VMEM((tq, 2 * tq), F32),
                        pltpu.VMEM((tq, 2 * tq), F32)],
        compiler_params=_params(2),
        name="prompt_attn",
    )(slopes, z_b, z_b, z_b, z_s, z_s, *lam_vecs, subln_g.reshape(HEAD_V, 1))


def _meta_attn_kernel(slope_ref, q_ref, k_ref, v_ref, lq1_ref, lk1_ref, lq2_ref, lk2_ref, g_ref,
                      o_ref, *, n_meta, lam_init):
    slope = slope_ref[pl.program_id(0)]
    q2 = _stack_maps(q_ref[...])
    row = lax.broadcasted_iota(jnp.int32, (2 * n_meta, n_meta), 0)
    col = lax.broadcasted_iota(jnp.int32, (2 * n_meta, n_meta), 1)
    row = jnp.where(row >= n_meta, row - n_meta, row)
    s = lax.dot_general(q2, k_ref[...].astype(BF16), _NT, preferred_element_type=F32)
    s = jnp.where(col <= row, s + slope * (col - row).astype(F32), MASK_VALUE)
    p = jnp.exp(s - jnp.max(s, axis=-1, keepdims=True))
    l = jnp.sum(p, axis=-1, keepdims=True)
    acc = jnp.dot(p.astype(BF16), v_ref[...].astype(BF16), preferred_element_type=F32)
    lam = _diff_lambda(lq1_ref, lk1_ref, lq2_ref, lk2_ref, lam_init)
    o_ref[...] = _diff_combine(acc, l, lam, g_ref[...], lam_init)


def _meta_attention(z_s, meta_row0, slopes, lam_vecs, subln_g, lam_init, n_meta):
    meta_blk = meta_row0 // n_meta
    vec = pl.BlockSpec((1, HEAD_QK), lambda h: (0, 0))
    kern = functools.partial(_meta_attn_kernel, n_meta=n_meta, lam_init=lam_init)
    return pl.pallas_call(
        kern,
        grid=(N_HEADS,),
        in_specs=[pl.BlockSpec(memory_space=pltpu.SMEM),
                  pl.BlockSpec((n_meta, HEAD_V), lambda h: (meta_blk, h)),
                  pl.BlockSpec((n_meta, HEAD_V), lambda h: (meta_blk, N_HEADS + h)),
                  pl.BlockSpec((n_meta, HEAD_V), lambda h: (meta_blk, 2 * N_HEADS + h)),
                  vec, vec, vec, vec,
                  pl.BlockSpec((1, HEAD_V), lambda h: (0, 0))],
        out_specs=pl.BlockSpec((n_meta, HEAD_V), lambda h: (0, h)),
        out_shape=jax.ShapeDtypeStruct((n_meta, D_ATTN), F32),
        compiler_params=_params(1),
        name="meta_attn",
    )(slopes, z_s, z_s, z_s, *lam_vecs, subln_g)


def _sample_attn_parts(slope_ref, q_ref, kn_ref, vn_ref, k_pages, v_pages, lam_refs, g_ref, o_ref,
                       q2_ref, m_ref, l_ref, acc_ref, bias_ref, *, page_size, ld, past, lam_init):
    lq1_ref, lk1_ref, lq2_ref, lk2_ref = lam_refs
    tk = len(k_pages) * page_size
    n_pairs = N_HEADS // 2
    hr = 2 * ld
    pr = 2 * hr
    rows = N_HEADS * hr
    pair_heads = [(j, j + n_pairs) for j in range(n_pairs)]

    slope_col = jnp.concatenate([jnp.full((hr, 1), slope_ref[h], F32)
                                 for pair in pair_heads for h in pair], axis=0)

    def row_head_and_query(shape):
        row = lax.broadcasted_iota(jnp.int32, shape, 0)
        return lax.rem(lax.div(row, hr), 2), lax.rem(row, ld)

    def prologue(first_step, seq_start):
        @pl.when(first_step)
        def _():
            rh, qi = row_head_and_query((rows, 2 * tk))
            col = lax.broadcasted_iota(jnp.int32, (rows, 2 * tk), 1)
            rel = (lax.div(col, 2) - qi).astype(F32)
            bias_ref[...] = jnp.where(lax.rem(col, 2) == rh, slope_col * rel, MASK_VALUE)

        @pl.when(seq_start)
        def _():
            for j, pair in enumerate(pair_heads):
                q2_ref[j] = jnp.concatenate(
                    [_stack_maps(q_ref[:, h * HEAD_V:(h + 1) * HEAD_V]) for h in pair], axis=0)
            m_ref[...] = jnp.full(m_ref.shape, MASK_VALUE, F32)
            l_ref[...] = jnp.zeros(l_ref.shape, F32)
            acc_ref[...] = jnp.zeros(acc_ref.shape, F32)

    def update(s, values):
        m_old = m_ref[...]
        m_new = jnp.maximum(m_old, jnp.max(s, axis=-1, keepdims=True))
        p = jnp.exp(s - m_new)
        alpha = jnp.exp(m_old - m_new)
        l_ref[...] = alpha * l_ref[...] + jnp.sum(p, axis=-1, keepdims=True)
        p = p.astype(BF16)
        pv = [jnp.dot(p[j * pr:(j + 1) * pr], values[j], preferred_element_type=F32)
              for j in range(n_pairs)]
        acc_ref[...] = alpha * acc_ref[...] + jnp.concatenate(pv, axis=0)
        m_ref[...] = m_new

    def pair_rows(pages, j):
        return jnp.concatenate([pg[0, pl.ds(j, 2 * page_size, stride=n_pairs), :] for pg in pages],
                               axis=0).astype(BF16)

    def main(c):
        s_parts, values = [], []
        for j in range(n_pairs):
            s_parts.append(lax.dot_general(q2_ref[j], pair_rows(k_pages, j), _NT,
                                           preferred_element_type=F32))
            values.append(pair_rows(v_pages, j))
        group_offset = slope_col * (c * tk - past).astype(F32)
        update(jnp.concatenate(s_parts, axis=0) + bias_ref[...] + group_offset, values)

    def epilogue(seq_end):
        @pl.when(seq_end)
        def _():
            lam = _diff_lambda(lq1_ref, lk1_ref, lq2_ref, lk2_ref, lam_init)
            rh, qi = row_head_and_query((rows, hr))
            col = lax.broadcasted_iota(jnp.int32, (rows, hr), 1)
            kj = lax.rem(col, ld)
            visible = (lax.div(col, ld) == rh) & (kj <= qi)
            s_parts, values = [], []
            for j, pair in enumerate(pair_heads):
                kn = jnp.concatenate([kn_ref[:, h * HEAD_V:(h + 1) * HEAD_V] for h in pair], axis=0)
                vn = jnp.concatenate([vn_ref[:, h * HEAD_V:(h + 1) * HEAD_V] for h in pair], axis=0)
                s_parts.append(lax.dot_general(q2_ref[j], kn.astype(BF16), _NT,
                                               preferred_element_type=F32))
                values.append(vn.astype(BF16))
            s = jnp.concatenate(s_parts, axis=0) + slope_col * (kj - qi).astype(F32)
            update(jnp.where(visible, s, MASK_VALUE), values)
            acc = acc_ref[...]
            l = l_ref[...]
            for j, pair in enumerate(pair_heads):
                for t, h in enumerate(pair):
                    r0 = j * pr + t * hr
                    o_ref[:, h * HEAD_V:(h + 1) * HEAD_V] = _diff_combine(
                        acc[r0:r0 + hr], l[r0:r0 + hr], lam, g_ref[...], lam_init)

    return prologue, main, epilogue


def _window_sums(ext):
    out = {}
    s = ext
    w = 1
    while w < max(POOL_WINDOWS):
        s = s + pltpu.roll(s, w, 0)
        w *= 2
        out[w] = s
    return out


def _pool_project(u, sums, inv_cnt, wp_ref, scale_ref):
    cg = u.shape[1] // len(POOL_WINDOWS)
    outs = []
    for g, w in enumerate(POOL_WINDOWS):
        sl = slice(g * cg, (g + 1) * cg)
        d = sums[w][:, sl] * inv_cnt[w] - u[:, sl]
        outs.append(jnp.dot(d.astype(BF16), wp_ref[g], preferred_element_type=F32))
    return jnp.concatenate(outs, axis=1) * scale_ref[...]


def _pool_big_kernel(u_ref, prev_ref, meta_ref, wp_ref, scale_ref, m_ref):
    halo = jnp.where(pl.program_id(0) == 0, meta_ref[...], prev_ref[...])
    u = u_ref[...]
    sums = _window_sums(jnp.concatenate([halo, u], axis=0))
    sums = {w: s[POOL_HALO:] for w, s in sums.items()}
    inv = {w: 1.0 / w for w in POOL_WINDOWS}
    m_ref[...] = _pool_project(u, sums, inv, wp_ref, scale_ref).astype(m_ref.dtype)


def _pool_big(z_b, z_s, meta_row0, w_pool, pool_scale, tm=512):
    seq = z_b.shape[0]
    d_pool = w_pool.shape[0] * w_pool.shape[1]
    tm = _row_tile(seq, tm)
    ucol = z_b.shape[1] // d_pool - 1
    per = tm // POOL_HALO
    return pl.pallas_call(
        _pool_big_kernel,
        grid=(seq // tm,),
        in_specs=[pl.BlockSpec((tm, d_pool), lambda i: (i, ucol)),
                  pl.BlockSpec((POOL_HALO, d_pool), lambda i: (jnp.maximum(i * per - 1, 0), ucol)),
                  pl.BlockSpec((POOL_HALO, d_pool), lambda i: (meta_row0 // POOL_HALO, ucol)),
                  pl.BlockSpec(w_pool.shape, lambda i: (0, 0, 0)),
                  pl.BlockSpec((1, d_pool), lambda i: (0, 0))],
        out_specs=pl.BlockSpec((tm, d_pool), lambda i: (i, 0)),
        out_shape=jax.ShapeDtypeStruct((seq, d_pool), BF16),
        compiler_params=_params(1),
        name="pool_big",
    )(z_b, z_b, z_s, w_pool, pool_scale)


def _pool_small_kernel(ext_ref, wp_ref, scale_ref, m_ref, *, bd, ld, n_meta):
    grp = POOL_HALO + ld
    ext = ext_ref[...]
    sums = _window_sums(ext)
    meta0 = bd * grp + POOL_HALO

    def new_rows(a):
        parts = [a[b * grp + POOL_HALO:(b + 1) * grp] for b in range(bd)]
        return jnp.concatenate(parts + [a[meta0:meta0 + n_meta]], axis=0)

    n = bd * ld + n_meta
    r = lax.broadcasted_iota(jnp.int32, (n, 1), 0)
    inv = {}
    for w in POOL_WINDOWS:
        cnt = jnp.where(r < bd * ld, w, jnp.minimum(w, r - bd * ld + 1))
        inv[w] = 1.0 / cnt.astype(F32)
    sums = {w: new_rows(s) for w, s in sums.items()}
    m_ref[...] = _pool_project(new_rows(ext), sums, inv, wp_ref, scale_ref).astype(m_ref.dtype)


def _pool_small(ext, w_pool, pool_scale, bd, ld, n_meta):
    n = bd * ld + n_meta
    d_pool = ext.shape[1]
    kern = functools.partial(_pool_small_kernel, bd=bd, ld=ld, n_meta=n_meta)
    return pl.pallas_call(
        kern,
        grid=(1,),
        in_specs=[pl.BlockSpec(ext.shape, lambda i: (0, 0)),
                  pl.BlockSpec(w_pool.shape, lambda i: (0, 0, 0)),
                  pl.BlockSpec((1, d_pool), lambda i: (0, 0))],
        out_specs=pl.BlockSpec((n, d_pool), lambda i: (0, 0)),
        out_shape=jax.ShapeDtypeStruct((n, d_pool), BF16),
        compiler_params=_params(1),
        name="pool_small",
    )(ext, w_pool, pool_scale)


def _out_proj_kernel(x_ref, a_ref, m_ref, wa_ref, wm_ref, h_ref):
    h_ref[...] = (x_ref[...]
                  + jnp.dot(a_ref[...].astype(BF16), wa_ref[...], preferred_element_type=F32)
                  + jnp.dot(m_ref[...], wm_ref[...], preferred_element_type=F32))


def _out_proj(x, a, m, w_out, tm=512):
    rows, d = x.shape
    da, dm = a.shape[1], m.shape[1]
    tm = _row_tile(rows, tm)
    return pl.pallas_call(
        _out_proj_kernel,
        grid=(rows // tm,),
        in_specs=[pl.BlockSpec((tm, d), lambda i: (i, 0)),
                  pl.BlockSpec((tm, da), lambda i: (i, 0)),
                  pl.BlockSpec((tm, dm), lambda i: (i, 0)),
                  pl.BlockSpec((da, d), lambda i: (0, 0)),
                  pl.BlockSpec((dm, d), lambda i: (da // dm, 0))],
        out_specs=pl.BlockSpec((tm, d), lambda i: (i, 0)),
        out_shape=jax.ShapeDtypeStruct((rows, d), F32),
        compiler_params=_params(1),
        name="out_proj",
    )(x, a, m, w_out, w_out)


def _mlp_parts(h_ref, g_ref, wu_ref, wd_ref, gf_ref, y_ref, xn_ref):
    j = pl.program_id(1)

    def prologue():
        @pl.when(j == 0)
        def _():
            h = h_ref[...]
            xn_ref[...] = _rmsnorm(h, g_ref[...]).astype(BF16)
            y_ref[...] = h

    def main():
        a = jnp.maximum(jnp.dot(xn_ref[...], wu_ref[...], preferred_element_type=F32), 0.0)
        y_ref[...] += jnp.dot((a * a).astype(BF16), wd_ref[...], preferred_element_type=F32)

    def epilogue():
        @pl.when(j == pl.num_programs(1) - 1)
        def _():
            y_ref[...] = _rmsnorm(y_ref[...], gf_ref[...])

    return prologue, main, epilogue


def _mlp_kernel(h_ref, g_ref, wu_ref, wd_ref, gf_ref, y_ref, xn_ref):
    prologue, main, epilogue = _mlp_parts(h_ref, g_ref, wu_ref, wd_ref, gf_ref, y_ref, xn_ref)
    prologue()
    main()
    epilogue()


def _mlp_sample_kernel(pt_ref, h_ref, g_ref, wu_ref, wd_ref, gf_ref, slope_ref, q_ref, kn_ref, vn_ref,
                       *rest, pages_per_step, steps_per_seq, n_sample_steps, page_size, ld, past,
                       lam_init):
    del pt_ref
    P = pages_per_step
    k_pages, v_pages = rest[:P], rest[P:2 * P]
    lam_refs = rest[2 * P:2 * P + 4]
    g_sub_ref, y_ref, o_ref, xn_ref, q2_ref, m_ref, l_ref, acc_ref, bias_ref = rest[2 * P + 4:]
    t = pl.program_id(0) * pl.num_programs(1) + pl.program_id(1)
    active = t < n_sample_steps
    c = lax.rem(jnp.minimum(t, n_sample_steps - 1), steps_per_seq)
    mlp_prologue, mlp_main, mlp_epilogue = _mlp_parts(h_ref, g_ref, wu_ref, wd_ref, gf_ref, y_ref,
                                                      xn_ref)
    att_prologue, att_main, att_epilogue = _sample_attn_parts(
        slope_ref, q_ref, kn_ref, vn_ref, k_pages, v_pages, lam_refs, g_sub_ref, o_ref,
        q2_ref, m_ref, l_ref, acc_ref, bias_ref, page_size=page_size, ld=ld, past=past,
        lam_init=lam_init)
    mlp_prologue()
    att_prologue(t == 0, active & (c == 0))
    mlp_main()
    att_main(c)
    mlp_epilogue()
    att_epilogue(active & (c == steps_per_seq - 1))


def _mlp_with_sample_attention(h, g, w_up, w_down, g_final, z_s, cache_k, cache_v, page_table, slopes,
                               lam_vecs, subln_g, lam_init, ld, tm=512, tf=512):
    rows, d = h.shape
    d_ff = w_up.shape[1]
    tm = _row_tile(rows, tm)
    n_i, n_j = rows // tm, d_ff // tf
    bd, n_pages = page_table.shape
    n_phys, page_size = cache_k.shape[0], cache_k.shape[1]
    past = n_pages * page_size
    P = min(p for p in range(1, n_pages + 1) if n_pages % p == 0 and bd * (n_pages // p) <= n_i * n_j)
    assert P <= MAX_PAGES_PER_STEP
    spp = n_pages // P
    n_steps = bd * spp
    k2 = cache_k.reshape(n_phys, page_size * N_HEADS, HEAD_V)
    v2 = cache_v.reshape(n_phys, page_size * N_HEADS, HEAD_V)

    def seq_and_group(i, j):
        t = jnp.minimum(i * n_j + j, n_steps - 1)
        return t // spp, t % spp

    def seq_spec(col):
        return pl.BlockSpec((ld, D_ATTN), lambda i, j, pt: (seq_and_group(i, j)[0], col))

    def page_spec(k):
        def index(i, j, pt):
            b, c = seq_and_group(i, j)
            return pt[b * n_pages + c * P + k], 0, 0
        return pl.BlockSpec((1, page_size * N_HEADS, HEAD_V), index)

    vec = pl.BlockSpec((1, HEAD_QK), lambda i, j, pt: (0, 0))
    kern = functools.partial(_mlp_sample_kernel, pages_per_step=P, steps_per_seq=spp,
                             n_sample_steps=n_steps, page_size=page_size, ld=ld, past=past,
                             lam_init=lam_init)
    score_rows = N_HEADS * 2 * ld
    grid_spec = pltpu.PrefetchScalarGridSpec(
        num_scalar_prefetch=1,
        grid=(n_i, n_j),
        in_specs=[pl.BlockSpec((tm, d), lambda i, j, pt: (i, 0)),
                  pl.BlockSpec((1, d), lambda i, j, pt: (0, 0)),
                  pl.BlockSpec((d, tf), lambda i, j, pt: (0, j)),
                  pl.BlockSpec((tf, d), lambda i, j, pt: (j, 0)),
                  pl.BlockSpec((1, d), lambda i, j, pt: (0, 0)),
                  pl.BlockSpec(memory_space=pltpu.SMEM),
                  seq_spec(0), seq_spec(1), seq_spec(2)]
                 + [page_spec(k) for k in range(P)] + [page_spec(k) for k in range(P)]
                 + [vec, vec, vec, vec, pl.BlockSpec((1, HEAD_V), lambda i, j, pt: (0, 0))],
        out_specs=[pl.BlockSpec((tm, d), lambda i, j, pt: (i, 0)),
                   pl.BlockSpec((ld, D_ATTN), lambda i, j, pt: (seq_and_group(i, j)[0], 0))],
        scratch_shapes=[pltpu.VMEM((tm, d), BF16),
                        pltpu.VMEM((N_HEADS // 2, 4 * ld, HEAD_V), BF16),
                        pltpu.VMEM((score_rows, 1), F32),
                        pltpu.VMEM((score_rows, 1), F32),
                        pltpu.VMEM((score_rows, HEAD_V), F32),
                        pltpu.VMEM((score_rows, 2 * P * page_size), F32)])
    return pl.pallas_call(
        kern,
        grid_spec=grid_spec,
        out_shape=[jax.ShapeDtypeStruct((rows, d), F32),
                   jax.ShapeDtypeStruct((bd * ld, D_ATTN), F32)],
        compiler_params=_params(2),
        name="mlp_sample_attn",
    )(page_table.reshape(-1), h, g, w_up, w_down, g_final, slopes, z_s, z_s, z_s,
      *([k2] * P), *([v2] * P), *lam_vecs, subln_g)


def _mlp(h, g, w_up, w_down, g_final, tm=512, tf=1024):
    rows, d = h.shape
    d_ff = w_up.shape[1]
    tm = _row_tile(rows, tm)
    return pl.pallas_call(
        _mlp_kernel,
        grid=(rows // tm, d_ff // tf),
        in_specs=[pl.BlockSpec((tm, d), lambda i, j: (i, 0)),
                  pl.BlockSpec((1, d), lambda i, j: (0, 0)),
                  pl.BlockSpec((d, tf), lambda i, j: (0, j)),
                  pl.BlockSpec((tf, d), lambda i, j: (j, 0)),
                  pl.BlockSpec((1, d), lambda i, j: (0, 0))],
        out_specs=pl.BlockSpec((tm, d), lambda i, j: (i, 0)),
        out_shape=jax.ShapeDtypeStruct((rows, d), F32),
        scratch_shapes=[pltpu.VMEM((tm, d), BF16)],
        compiler_params=_params(2),
        name="mlp",
    )(h, g, w_up, w_down, g_final)


def kernel(x_prompt, x_sample, cache_k, cache_v, state_pool, page_table, meta_tokens, norm_mix_g,
           w_in, lambda_q1, lambda_k1, lambda_q2, lambda_k2, subln_g, w_pool, pool_scale, w_out,
           norm_mlp_g, w_up, w_down, norm_final_g):
    depth = norm_mix_g.shape[0]
    batch, seq, d_model = x_prompt.shape
    bd, ld, _ = x_sample.shape
    n_meta = meta_tokens.shape[0]
    assert depth == 1 and batch == 1, "one layer and one prompt sequence are supported"
    assert n_meta == POOL_HALO and (bd * ld) % n_meta == 0 and seq >= POOL_HALO
    layer = 0
    lam_init = 0.8 - 0.6 * math.exp(-0.3 * layer)
    slopes = 2.0 ** (-8.0 * jnp.arange(1, N_HEADS + 1, dtype=F32) / N_HEADS)
    n_small = bd * ld
    d_pool = w_pool.shape[1] * w_pool.shape[2]

    x_big = x_prompt.reshape(seq, d_model)
    x_small = jnp.concatenate([x_sample.reshape(n_small, d_model), meta_tokens.astype(F32)], axis=0)

    g_mix = norm_mix_g[layer].reshape(1, d_model)
    g_mlp = norm_mlp_g[layer].reshape(1, d_model)
    g_final = norm_final_g.reshape(1, d_model)
    g_sub = subln_g[layer].reshape(1, HEAD_V)
    scale_pool = pool_scale[layer].reshape(1, d_pool)
    lam_vecs = [v[layer].reshape(1, HEAD_QK) for v in (lambda_q1, lambda_k1, lambda_q2, lambda_k2)]
    w_in_b = w_in[layer].astype(BF16)
    w_out_b = w_out[layer].astype(BF16)
    w_up_b = w_up[layer].astype(BF16)
    w_down_b = w_down[layer].astype(BF16)
    w_pool_b = w_pool[layer].astype(BF16)

    z_b = _in_proj(x_big, g_mix, w_in_b)
    z_s = _in_proj(x_small, g_mix, w_in_b)

    a_b = _prompt_attention(z_b, z_s, n_small, slopes, lam_vecs, g_sub, lam_init, n_meta)
    a_meta = _meta_attention(z_s, n_small, slopes, lam_vecs, g_sub, lam_init, n_meta)

    u_b = z_b[:, 3 * D_ATTN:]
    u_s = z_s[:n_small, 3 * D_ATTN:].reshape(bd, ld, d_pool)
    u_meta = z_s[n_small:, 3 * D_ATTN:]
    hist = state_pool[layer].astype(F32)
    n_hist = hist.shape[1]
    zpad = jnp.zeros((bd, POOL_HALO - n_hist, d_pool), F32)
    ext = jnp.concatenate([jnp.concatenate([zpad, hist, u_s], axis=1).reshape(-1, d_pool),
                           jnp.zeros((POOL_HALO, d_pool), F32), u_meta], axis=0)
    m_b = _pool_big(z_b, z_s, n_small, w_pool_b, scale_pool)
    m_s = _pool_small(ext, w_pool_b, scale_pool, bd, ld, n_meta)

    h_b = _out_proj(x_big, a_b, m_b, w_out_b)
    y_b, a_samp = _mlp_with_sample_attention(h_b, g_mlp, w_up_b, w_down_b, g_final, z_s, cache_k[layer],
                                             cache_v[layer], page_table, slopes, lam_vecs, g_sub,
                                             lam_init, ld)
    a_s = jnp.concatenate([a_samp, a_meta], axis=0)
    h_s = _out_proj(x_small, a_s, m_s, w_out_b)
    y_s = _mlp(h_s, g_mlp, w_up_b, w_down_b, g_final)

    def heads(z, col):
        return z[:, col * D_ATTN:(col + 1) * D_ATTN].reshape(-1, N_HEADS, HEAD_V)

    t = seq + n_meta
    k_prompt = jnp.concatenate([heads(z_s[n_small:], 1), heads(z_b, 1)], axis=0)
    v_prompt = jnp.concatenate([heads(z_s[n_small:], 2), heads(z_b, 2)], axis=0)
    u_ext = jnp.concatenate([hist, u_s], axis=1)
    return (y_b.reshape(1, seq, d_model),
            y_s[:n_small].reshape(bd, ld, d_model),
            k_prompt.reshape(1, 1, t, N_HEADS, HEAD_V),
            v_prompt.reshape(1, 1, t, N_HEADS, HEAD_V),
            u_b[-n_hist:].reshape(1, 1, n_hist, d_pool),
            heads(z_s[:n_small], 1).reshape(1, bd, ld, N_HEADS, HEAD_V),
            heads(z_s[:n_small], 2).reshape(1, bd, ld, N_HEADS, HEAD_V),
            u_ext[:, -n_hist:].reshape(1, bd, n_hist, d_pool))
```

```python
import functools
import math

import jax
import jax.numpy as jnp
from jax import lax
from jax.experimental import pallas as pl
from jax.experimental.pallas import tpu as pltpu

N_HEADS = 8
HEAD_V = 128
HEAD_QK = HEAD_V // 2
D_ATTN = N_HEADS * HEAD_V
POOL_WINDOWS = (2, 4, 8, 16)
POOL_HALO = 16
EPS = 1e-6
MASK_VALUE = -1e30
QK_SCALE = HEAD_QK ** -0.5
LOG2E = 1.4426950408889634
POS_RADIX = 32
ACC_ROWS = HEAD_V + 16

VMEM_LIMIT_BYTES = 56 * 1024 * 1024
MAX_PAGES_PER_STEP = 16
MLP_ROW_TILES_PER_WEIGHT_FETCH = 2
BF16 = jnp.bfloat16
F32 = jnp.float32

_NT = (((1,), (1,)), ((), ()))


def _params(n_grid_dims):
    return pltpu.CompilerParams(dimension_semantics=("arbitrary",) * n_grid_dims,
                                vmem_limit_bytes=VMEM_LIMIT_BYTES)


def _row_tile(rows, target):
    best = rows
    for t in range(16, min(rows, target) + 1, 16):
        if rows % t == 0:
            best = t
    return best if best <= target else rows


def _rmsnorm(x, g):
    ms = jnp.mean(x * x, axis=-1, keepdims=True)
    return x * lax.rsqrt(ms + EPS) * g


def _in_proj_kernel(x_ref, g_ref, w_ref, z_ref, xn_ref):
    @pl.when(pl.program_id(1) == 0)
    def _():
        xn_ref[...] = _rmsnorm(x_ref[...], g_ref[...]).astype(BF16)

    z_ref[...] = jnp.dot(xn_ref[...], w_ref[...], preferred_element_type=F32)


def _in_proj(x, g, w, tm_target=512, tn=2048):
    rows, d = x.shape
    n = w.shape[1]
    tm = _row_tile(rows, tm_target)
    return pl.pallas_call(
        _in_proj_kernel,
        grid=(rows // tm, n // tn),
        in_specs=[pl.BlockSpec((tm, d), lambda i, j: (i, 0)),
                  pl.BlockSpec((1, d), lambda i, j: (0, 0)),
                  pl.BlockSpec((d, tn), lambda i, j: (0, j))],
        out_specs=pl.BlockSpec((tm, tn), lambda i, j: (i, j)),
        out_shape=jax.ShapeDtypeStruct((rows, n), F32),
        scratch_shapes=[pltpu.VMEM((tm, d), BF16)],
        compiler_params=_params(2),
        name="in_proj",
    )(x, g, w)


def _stack_maps(q, scale=QK_SCALE):
    q = q * scale
    lane = lax.broadcasted_iota(jnp.int32, q.shape, 1)
    q1 = jnp.where(lane < HEAD_QK, q, 0.0)
    q2 = jnp.where(lane >= HEAD_QK, q, 0.0)
    return jnp.concatenate([q1, q2], axis=0).astype(BF16)


def _diff_lambda(lq1_ref, lk1_ref, lq2_ref, lk2_ref, lam_init):
    a = jnp.sum(lq1_ref[...] * lk1_ref[...], axis=-1, keepdims=True)
    b = jnp.sum(lq2_ref[...] * lk2_ref[...], axis=-1, keepdims=True)
    return jnp.exp(a) - jnp.exp(b) + lam_init


def _diff_combine(acc, l, lam, g, lam_init):
    n = acc.shape[0] // 2
    o = acc[:n] / l[:n] - lam * (acc[n:] / l[n:])
    return _rmsnorm(o, g) * (1.0 - lam_init)


def _prompt_attn_kernel(slope_ref, q_ref, k_ref, v_ref, km_ref, vm_ref,
                        lq1_ref, lk1_ref, lq2_ref, lk2_ref, gcol_ref, o_ref,
                        kb_ref, vt_ref, kmb_ref, vmt_ref, q2_ref, m_ref, acc_ref,
                        mask_ref, sa_ref, sb_ref, *, tq, n_meta, lam_init):
    h = pl.program_id(0)
    i = pl.program_id(1)
    slope = slope_ref[h] * LOG2E
    n_chunks = kb_ref.shape[0]

    @pl.when(i == 0)
    def _():
        key = lax.broadcasted_iota(jnp.int32, (tq, HEAD_V), 0)
        lane = lax.broadcasted_iota(jnp.int32, (tq, HEAD_V), 1)
        kfeat = jnp.where(lane < 3, lax.div(key, POS_RADIX),
                          jnp.where(lane < 6, lax.rem(key, POS_RADIX),
                                    jnp.where(lane < 9, 1, 0))).astype(BF16)
        sub = lax.broadcasted_iota(jnp.int32, (ACC_ROWS - HEAD_V, tq), 0)
        ones_row = jnp.where(sub == 0, 1.0, 0.0).astype(BF16)
        for c in range(n_chunks):
            kb_ref[c, :, :HEAD_V] = k_ref[c * tq:(c + 1) * tq, :].astype(BF16)
            kb_ref[c, :, HEAD_V:] = kfeat
            vt_ref[c, :HEAD_V, :] = v_ref[c * tq:(c + 1) * tq, :].T.astype(BF16)
            vt_ref[c, HEAD_V:, :] = ones_row
        pad = jnp.zeros((HEAD_V - n_meta, HEAD_V), F32)
        kmb_ref[:, :HEAD_V] = jnp.concatenate([km_ref[...], pad], axis=0).astype(BF16)
        kmb_ref[:, HEAD_V:] = jnp.zeros((HEAD_V, HEAD_V), BF16)
        vmt_ref[:HEAD_V, :] = jnp.concatenate([vm_ref[...], pad], axis=0).T.astype(BF16)
        vmt_ref[HEAD_V:, :] = ones_row[:, :HEAD_V]
        key = lax.broadcasted_iota(jnp.int32, (tq, 2 * tq), 0)
        qry = lax.broadcasted_iota(jnp.int32, (tq, 2 * tq), 1)
        qry = jnp.where(qry >= tq, qry - tq, qry)
        mask_ref[...] = jnp.where(key <= qry, 0.0, MASK_VALUE)
        qrow = lax.broadcasted_iota(jnp.int32, (2 * tq, HEAD_V), 0)
        lane = lax.broadcasted_iota(jnp.int32, (2 * tq, HEAD_V), 1)
        qrow = jnp.where(qrow >= tq, qrow - tq, qrow).astype(F32)
        whole = jnp.where(lane < 3, slope * POS_RADIX, jnp.where(lane < 6, slope, -slope * qrow))
        piece1 = whole.astype(BF16).astype(F32)
        rest = whole - piece1
        piece2 = rest.astype(BF16).astype(F32)
        piece3 = rest - piece2
        third = lax.rem(lane, 3)
        qfeat = jnp.where(third == 0, piece1, jnp.where(third == 1, piece2, piece3))
        q2_ref[:, HEAD_V:] = jnp.where(lane < 9, qfeat, 0.0).astype(BF16)

    q2_ref[:, :HEAD_V] = _stack_maps(q_ref[...], QK_SCALE * LOG2E)

    def scores(c, s_ref):
        s_ref[...] = lax.dot_general(kb_ref[c], q2_ref[...], _NT, preferred_element_type=F32)

    scores(0, sa_ref)

    key = lax.broadcasted_iota(jnp.int32, (HEAD_V, 2 * tq), 0)
    qry = lax.broadcasted_iota(jnp.int32, (HEAD_V, 2 * tq), 1)
    qry = jnp.where(qry >= tq, qry - tq, qry)
    qpos = n_meta + i * tq + qry
    s = lax.dot_general(kmb_ref[...], q2_ref[...], _NT, preferred_element_type=F32)
    s = jnp.where(key < n_meta, s + slope * (key - qpos).astype(F32), MASK_VALUE)
    m0 = jnp.max(s, axis=0, keepdims=True)
    m_ref[...] = m0
    acc_ref[...] = jnp.dot(vmt_ref[...], jnp.exp2(s - m0).astype(BF16), preferred_element_type=F32)

    def softmax_update(c, s_ref, diagonal):
        s = s_ref[...]
        if diagonal:
            s = s + mask_ref[...]
        sigma = slope * (tq * (c - i)).astype(F32)
        m_old = m_ref[...]
        m_new = jnp.maximum(m_old, jnp.max(s, axis=0, keepdims=True) + sigma)
        p = jnp.exp2(s - (m_new - sigma)).astype(BF16)
        alpha = jnp.exp2(m_old - m_new)
        acc_ref[...] = alpha * acc_ref[...] + jnp.dot(vt_ref[c], p, preferred_element_type=F32)
        m_ref[...] = m_new

    def pair(j, carry):
        c = 2 * j
        scores(c + 1, sb_ref)
        softmax_update(c, sa_ref, False)
        scores(c + 2, sa_ref)
        softmax_update(c + 1, sb_ref, False)
        return carry

    lax.fori_loop(0, lax.shift_right_logical(i, 1), pair, 0)

    @pl.when((i & 1) == 0)
    def _():
        softmax_update(i, sa_ref, True)

    @pl.when((i & 1) == 1)
    def _():
        scores(i, sb_ref)
        softmax_update(i - 1, sa_ref, False)
        softmax_update(i, sb_ref, True)

    lam = _diff_lambda(lq1_ref, lk1_ref, lq2_ref, lk2_ref, lam_init)
    acc = acc_ref[:HEAD_V, :]
    l = acc_ref[HEAD_V:HEAD_V + 1, :]
    o = acc[:, :tq] / l[:, :tq] - lam * (acc[:, tq:] / l[:, tq:])
    ms = jnp.mean(o * o, axis=0, keepdims=True)
    o = o * lax.rsqrt(ms + EPS) * gcol_ref[...] * (1.0 - lam_init)
    o_ref[...] = o.T.astype(o_ref.dtype)


def _prompt_attention(z_b, z_s, meta_row0, slopes, lam_vecs, subln_g, lam_init, n_meta, tq=512):
    seq = z_b.shape[0]
    tq = _row_tile(seq, tq)
    assert tq % 128 == 0 and meta_row0 % n_meta == 0 and n_meta <= HEAD_V
    assert tq <= POS_RADIX * POS_RADIX, "key index digits must be exact in bf16"
    meta_blk = meta_row0 // n_meta
    vec = pl.BlockSpec((1, HEAD_QK), lambda h, i: (0, 0))
    kern = functools.partial(_prompt_attn_kernel, tq=tq, n_meta=n_meta, lam_init=lam_init)
    return pl.pallas_call(
        kern,
        grid=(N_HEADS, seq // tq),
        in_specs=[pl.BlockSpec(memory_space=pltpu.SMEM),
                  pl.BlockSpec((tq, HEAD_V), lambda h, i: (i, h)),
                  pl.BlockSpec((seq, HEAD_V), lambda h, i: (0, N_HEADS + h)),
                  pl.BlockSpec((seq, HEAD_V), lambda h, i: (0, 2 * N_HEADS + h)),
                  pl.BlockSpec((n_meta, HEAD_V), lambda h, i: (meta_blk, N_HEADS + h)),
                  pl.BlockSpec((n_meta, HEAD_V), lambda h, i: (meta_blk, 2 * N_HEADS + h)),
                  vec, vec, vec, vec,
                  pl.BlockSpec((HEAD_V, 1), lambda h, i: (0, 0))],
        out_specs=pl.BlockSpec((tq, HEAD_V), lambda h, i: (i, h)),
        out_shape=jax.ShapeDtypeStruct((seq, D_ATTN), BF16),
        scratch_shapes=[pltpu.VMEM((seq // tq, tq, 2 * HEAD_V), BF16),
                        pltpu.VMEM((seq // tq, ACC_ROWS, tq), BF16),
                        pltpu.VMEM((HEAD_V, 2 * HEAD_V), BF16),
                        pltpu.VMEM((ACC_ROWS, HEAD_V), BF16),
                        pltpu.VMEM((2 * tq, 2 * HEAD_V), BF16),
                        pltpu.VMEM((1, 2 * tq), F32),
                        pltpu.VMEM((ACC_ROWS, 2 * tq), F32),
                        pltpu.VMEM((tq, 2 * tq), F32),
                        pltpu.VMEM((tq, 2 * tq), F32),
                        pltpu.VMEM((tq, 2 * tq), F32)],
        compiler_params=_params(2),
        name="prompt_attn",
    )(slopes, z_b, z_b, z_b, z_s, z_s, *lam_vecs, subln_g.reshape(HEAD_V, 1))


def _meta_attn_kernel(slope_ref, q_ref, k_ref, v_ref, lq1_ref, lk1_ref, lq2_ref, lk2_ref, g_ref,
                      o_ref, *, n_meta, lam_init):
    slope = slope_ref[pl.program_id(0)]
    q2 = _stack_maps(q_ref[...])
    row = lax.broadcasted_iota(jnp.int32, (2 * n_meta, n_meta), 0)
    col = lax.broadcasted_iota(jnp.int32, (2 * n_meta, n_meta), 1)
    row = jnp.where(row >= n_meta, row - n_meta, row)
    s = lax.dot_general(q2, k_ref[...].astype(BF16), _NT, preferred_element_type=F32)
    s = jnp.where(col <= row, s + slope * (col - row).astype(F32), MASK_VALUE)
    p = jnp.exp(s - jnp.max(s, axis=-1, keepdims=True))
    l = jnp.sum(p, axis=-1, keepdims=True)
    acc = jnp.dot(p.astype(BF16), v_ref[...].astype(BF16), preferred_element_type=F32)
    lam = _diff_lambda(lq1_ref, lk1_ref, lq2_ref, lk2_ref, lam_init)
    o_ref[...] = _diff_combine(acc, l, lam, g_ref[...], lam_init)


def _meta_attention(z_s, meta_row0, slopes, lam_vecs, subln_g, lam_init, n_meta):
    meta_blk = meta_row0 // n_meta
    vec = pl.BlockSpec((1, HEAD_QK), lambda h: (0, 0))
    kern = functools.partial(_meta_attn_kernel, n_meta=n_meta, lam_init=lam_init)
    return pl.pallas_call(
        kern,
        grid=(N_HEADS,),
        in_specs=[pl.BlockSpec(memory_space=pltpu.SMEM),
                  pl.BlockSpec((n_meta, HEAD_V), lambda h: (meta_blk, h)),
                  pl.BlockSpec((n_meta, HEAD_V), lambda h: (meta_blk, N_HEADS + h)),
                  pl.BlockSpec((n_meta, HEAD_V), lambda h: (meta_blk, 2 * N_HEADS + h)),
                  vec, vec, vec, vec,
                  pl.BlockSpec((1, HEAD_V), lambda h: (0, 0))],
        out_specs=pl.BlockSpec((n_meta, HEAD_V), lambda h: (0, h)),
        out_shape=jax.ShapeDtypeStruct((n_meta, D_ATTN), F32),
        compiler_params=_params(1),
        name="meta_attn",
    )(slopes, z_s, z_s, z_s, *lam_vecs, subln_g)


def _sample_attn_parts(slope_ref, q_ref, kn_ref, vn_ref, k_pages, v_pages, lam_refs, g_ref, o_ref,
                       q2_ref, m_ref, l_ref, acc_ref, bias_ref, *, page_size, ld, past, lam_init):
    lq1_ref, lk1_ref, lq2_ref, lk2_ref = lam_refs
    tk = len(k_pages) * page_size
    n_pairs = N_HEADS // 2
    hr = 2 * ld
    pr = 2 * hr
    rows = N_HEADS * hr
    pair_heads = [(j, j + n_pairs) for j in range(n_pairs)]

    slope_col = jnp.concatenate([jnp.full((hr, 1), slope_ref[h], F32)
                                 for pair in pair_heads for h in pair], axis=0)

    def row_head_and_query(shape):
        row = lax.broadcasted_iota(jnp.int32, shape, 0)
        return lax.rem(lax.div(row, hr), 2), lax.rem(row, ld)

    def prologue(first_step, seq_start):
        @pl.when(first_step)
        def _():
            rh, qi = row_head_and_query((rows, 2 * tk))
            col = lax.broadcasted_iota(jnp.int32, (rows, 2 * tk), 1)
            rel = (lax.div(col, 2) - qi).astype(F32)
            bias_ref[...] = jnp.where(lax.rem(col, 2) == rh, slope_col * rel, MASK_VALUE)

        @pl.when(seq_start)
        def _():
            for j, pair in enumerate(pair_heads):
                q2_ref[j] = jnp.concatenate(
                    [_stack_maps(q_ref[:, h * HEAD_V:(h + 1) * HEAD_V]) for h in pair], axis=0)
            m_ref[...] = jnp.full(m_ref.shape, MASK_VALUE, F32)
            l_ref[...] = jnp.zeros(l_ref.shape, F32)
            acc_ref[...] = jnp.zeros(acc_ref.shape, F32)

    def update(s, values):
        m_old = m_ref[...]
        m_new = jnp.maximum(m_old, jnp.max(s, axis=-1, keepdims=True))
        p = jnp.exp(s - m_new)
        alpha = jnp.exp(m_old - m_new)
        l_ref[...] = alpha * l_ref[...] + jnp.sum(p, axis=-1, keepdims=True)
        p = p.astype(BF16)
        pv = [jnp.dot(p[j * pr:(j + 1) * pr], values[j], preferred_element_type=F32)
              for j in range(n_pairs)]
        acc_ref[...] = alpha * acc_ref[...] + jnp.concatenate(pv, axis=0)
        m_ref[...] = m_new

    def pair_rows(pages, j):
        return jnp.concatenate([pg[0, pl.ds(j, 2 * page_size, stride=n_pairs), :] for pg in pages],
                               axis=0).astype(BF16)

    def main(c):
        s_parts, values = [], []
        for j in range(n_pairs):
            s_parts.append(lax.dot_general(q2_ref[j], pair_rows(k_pages, j), _NT,
                                           preferred_element_type=F32))
            values.append(pair_rows(v_pages, j))
        group_offset = slope_col * (c * tk - past).astype(F32)
        update(jnp.concatenate(s_parts, axis=0) + bias_ref[...] + group_offset, values)

    def epilogue(seq_end):
        @pl.when(seq_end)
        def _():
            lam = _diff_lambda(lq1_ref, lk1_ref, lq2_ref, lk2_ref, lam_init)
            rh, qi = row_head_and_query((rows, hr))
            col = lax.broadcasted_iota(jnp.int32, (rows, hr), 1)
            kj = lax.rem(col, ld)
            visible = (lax.div(col, ld) == rh) & (kj <= qi)
            s_parts, values = [], []
            for j, pair in enumerate(pair_heads):
                kn = jnp.concatenate([kn_ref[:, h * HEAD_V:(h + 1) * HEAD_V] for h in pair], axis=0)
                vn = jnp.concatenate([vn_ref[:, h * HEAD_V:(h + 1) * HEAD_V] for h in pair], axis=0)
                s_parts.append(lax.dot_general(q2_ref[j], kn.astype(BF16), _NT,
                                               preferred_element_type=F32))
                values.append(vn.astype(BF16))
            s = jnp.concatenate(s_parts, axis=0) + slope_col * (kj - qi).astype(F32)
            update(jnp.where(visible, s, MASK_VALUE), values)
            acc = acc_ref[...]
            l = l_ref[...]
            for j, pair in enumerate(pair_heads):
                for t, h in enumerate(pair):
                    r0 = j * pr + t * hr
                    o_ref[:, h * HEAD_V:(h + 1) * HEAD_V] = _diff_combine(
                        acc[r0:r0 + hr], l[r0:r0 + hr], lam, g_ref[...], lam_init)

    return prologue, main, epilogue


def _window_sums(ext):
    out = {}
    s = ext
    w = 1
    while w < max(POOL_WINDOWS):
        s = s + pltpu.roll(s, w, 0)
        w *= 2
        out[w] = s
    return out


def _pool_project(u, sums, inv_cnt, wp_ref, scale_ref):
    cg = u.shape[1] // len(POOL_WINDOWS)
    outs = []
    for g, w in enumerate(POOL_WINDOWS):
        sl = slice(g * cg, (g + 1) * cg)
        d = sums[w][:, sl] * inv_cnt[w] - u[:, sl]
        outs.append(jnp.dot(d.astype(BF16), wp_ref[g], preferred_element_type=F32))
    return jnp.concatenate(outs, axis=1) * scale_ref[...]


def _pool_big_kernel(u_ref, prev_ref, meta_ref, wp_ref, scale_ref, m_ref):
    halo = jnp.where(pl.program_id(0) == 0, meta_ref[...], prev_ref[...])
    u = u_ref[...]
    sums = _window_sums(jnp.concatenate([halo, u], axis=0))
    sums = {w: s[POOL_HALO:] for w, s in sums.items()}
    inv = {w: 1.0 / w for w in POOL_WINDOWS}
    m_ref[...] = _pool_project(u, sums, inv, wp_ref, scale_ref).astype(m_ref.dtype)


def _pool_big(z_b, z_s, meta_row0, w_pool, pool_scale, tm=512):
    seq = z_b.shape[0]
    d_pool = w_pool.shape[0] * w_pool.shape[1]
    tm = _row_tile(seq, tm)
    ucol = z_b.shape[1] // d_pool - 1
    per = tm // POOL_HALO
    return pl.pallas_call(
        _pool_big_kernel,
        grid=(seq // tm,),
        in_specs=[pl.BlockSpec((tm, d_pool), lambda i: (i, ucol)),
                  pl.BlockSpec((POOL_HALO, d_pool), lambda i: (jnp.maximum(i * per - 1, 0), ucol)),
                  pl.BlockSpec((POOL_HALO, d_pool), lambda i: (meta_row0 // POOL_HALO, ucol)),
                  pl.BlockSpec(w_pool.shape, lambda i: (0, 0, 0)),
                  pl.BlockSpec((1, d_pool), lambda i: (0, 0))],
        out_specs=pl.BlockSpec((tm, d_pool), lambda i: (i, 0)),
        out_shape=jax.ShapeDtypeStruct((seq, d_pool), BF16),
        compiler_params=_params(1),
        name="pool_big",
    )(z_b, z_b, z_s, w_pool, pool_scale)


def _pool_small_kernel(ext_ref, wp_ref, scale_ref, m_ref, *, bd, ld, n_meta):
    grp = POOL_HALO + ld
    ext = ext_ref[...]
    sums = _window_sums(ext)
    meta0 = bd * grp + POOL_HALO

    def new_rows(a):
        parts = [a[b * grp + POOL_HALO:(b + 1) * grp] for b in range(bd)]
        return jnp.concatenate(parts + [a[meta0:meta0 + n_meta]], axis=0)

    n = bd * ld + n_meta
    r = lax.broadcasted_iota(jnp.int32, (n, 1), 0)
    inv = {}
    for w in POOL_WINDOWS:
        cnt = jnp.where(r < bd * ld, w, jnp.minimum(w, r - bd * ld + 1))
        inv[w] = 1.0 / cnt.astype(F32)
    sums = {w: new_rows(s) for w, s in sums.items()}
    m_ref[...] = _pool_project(new_rows(ext), sums, inv, wp_ref, scale_ref).astype(m_ref.dtype)


def _pool_small(ext, w_pool, pool_scale, bd, ld, n_meta):
    n = bd * ld + n_meta
    d_pool = ext.shape[1]
    kern = functools.partial(_pool_small_kernel, bd=bd, ld=ld, n_meta=n_meta)
    return pl.pallas_call(
        kern,
        grid=(1,),
        in_specs=[pl.BlockSpec(ext.shape, lambda i: (0, 0)),
                  pl.BlockSpec(w_pool.shape, lambda i: (0, 0, 0)),
                  pl.BlockSpec((1, d_pool), lambda i: (0, 0))],
        out_specs=pl.BlockSpec((n, d_pool), lambda i: (0, 0)),
        out_shape=jax.ShapeDtypeStruct((n, d_pool), BF16),
        compiler_params=_params(1),
        name="pool_small",
    )(ext, w_pool, pool_scale)


def _out_proj_kernel(x_ref, a_ref, m_ref, wa_ref, wm_ref, h_ref):
    h_ref[...] = (x_ref[...]
                  + jnp.dot(a_ref[...].astype(BF16), wa_ref[...], preferred_element_type=F32)
                  + jnp.dot(m_ref[...], wm_ref[...], preferred_element_type=F32))


def _out_proj(x, a, m, w_out, tm=512):
    rows, d = x.shape
    da, dm = a.shape[1], m.shape[1]
    tm = _row_tile(rows, tm)
    return pl.pallas_call(
        _out_proj_kernel,
        grid=(rows // tm,),
        in_specs=[pl.BlockSpec((tm, d), lambda i: (i, 0)),
                  pl.BlockSpec((tm, da), lambda i: (i, 0)),
                  pl.BlockSpec((tm, dm), lambda i: (i, 0)),
                  pl.BlockSpec((da, d), lambda i: (0, 0)),
                  pl.BlockSpec((dm, d), lambda i: (da // dm, 0))],
        out_specs=pl.BlockSpec((tm, d), lambda i: (i, 0)),
        out_shape=jax.ShapeDtypeStruct((rows, d), F32),
        compiler_params=_params(1),
        name="out_proj",
    )(x, a, m, w_out, w_out)


def _mlp_parts(h_ref, g_ref, wu_ref, wd_ref, gf_ref, y_ref, xn_ref, acc_ref):
    j = pl.program_id(1)

    def prologue():
        @pl.when(j == 0)
        def _():
            h = h_ref[...]
            xn_ref[...] = _rmsnorm(h, g_ref[...]).astype(BF16)
            acc_ref[...] = h

    def main():
        a = jnp.maximum(jnp.dot(xn_ref[...], wu_ref[...], preferred_element_type=F32), 0.0)
        acc_ref[...] += jnp.dot((a * a).astype(BF16), wd_ref[...], preferred_element_type=F32)

    def epilogue():
        @pl.when(j == pl.num_programs(1) - 1)
        def _():
            y_ref[...] = _rmsnorm(acc_ref[...], gf_ref[...])

    return prologue, main, epilogue


def _mlp_kernel(h_ref, g_ref, wu_ref, wd_ref, gf_ref, y_ref, xn_ref):
    prologue, main, epilogue = _mlp_parts(h_ref, g_ref, wu_ref, wd_ref, gf_ref, y_ref, xn_ref, y_ref)
    prologue()
    main()
    epilogue()


def _mlp_sample_kernel(pt_ref, h_ref, g_ref, wu_ref, wd_ref, gf_ref, slope_ref, q_ref, kn_ref, vn_ref,
                       *rest, pages_per_step, steps_per_seq, n_sample_steps, page_size, ld, past,
                       lam_init):
    del pt_ref
    P = pages_per_step
    k_pages, v_pages = rest[:P], rest[P:2 * P]
    lam_refs = rest[2 * P:2 * P + 4]
    (g_sub_ref, y_ref, o_ref, xn_ref, yacc_ref,
     q2_ref, m_ref, l_ref, acc_ref, bias_ref) = rest[2 * P + 4:]
    r = pl.program_id(2)
    t = (pl.program_id(0) * pl.num_programs(1) + pl.program_id(1)) * pl.num_programs(2) + r
    active = t < n_sample_steps
    c = lax.rem(jnp.minimum(t, n_sample_steps - 1), steps_per_seq)
    mlp_prologue, mlp_main, mlp_epilogue = _mlp_parts(h_ref, g_ref, wu_ref, wd_ref, gf_ref, y_ref,
                                                      xn_ref.at[r], yacc_ref.at[r])
    att_prologue, att_main, att_epilogue = _sample_attn_parts(
        slope_ref, q_ref, kn_ref, vn_ref, k_pages, v_pages, lam_refs, g_sub_ref, o_ref,
        q2_ref, m_ref, l_ref, acc_ref, bias_ref, page_size=page_size, ld=ld, past=past,
        lam_init=lam_init)
    mlp_prologue()
    att_prologue(t == 0, active & (c == 0))
    mlp_main()
    att_main(c)
    mlp_epilogue()
    att_epilogue(active & (c == steps_per_seq - 1))


def _mlp_with_sample_attention(h, g, w_up, w_down, g_final, z_s, cache_k, cache_v, page_table, slopes,
                               lam_vecs, subln_g, lam_init, ld, tm=512, tf=512):
    rows, d = h.shape
    d_ff = w_up.shape[1]
    tm = _row_tile(rows, tm)
    R = MLP_ROW_TILES_PER_WEIGHT_FETCH if (rows // tm) % MLP_ROW_TILES_PER_WEIGHT_FETCH == 0 else 1
    n_i, n_j = rows // (tm * R), d_ff // tf
    bd, n_pages = page_table.shape
    n_phys, page_size = cache_k.shape[0], cache_k.shape[1]
    past = n_pages * page_size
    P = min(p for p in range(1, n_pages + 1)
            if n_pages % p == 0 and bd * (n_pages // p) <= n_i * n_j * R)
    assert P <= MAX_PAGES_PER_STEP
    spp = n_pages // P
    n_steps = bd * spp
    k2 = cache_k.reshape(n_phys, page_size * N_HEADS, HEAD_V)
    v2 = cache_v.reshape(n_phys, page_size * N_HEADS, HEAD_V)

    def seq_and_group(i, j, r):
        t = jnp.minimum((i * n_j + j) * R + r, n_steps - 1)
        return t // spp, t % spp

    def seq_spec(col):
        return pl.BlockSpec((ld, D_ATTN), lambda i, j, r, pt: (seq_and_group(i, j, r)[0], col))

    def page_spec(k):
        def index(i, j, r, pt):
            b, c = seq_and_group(i, j, r)
            return pt[b * n_pages + c * P + k], 0, 0
        return pl.BlockSpec((1, page_size * N_HEADS, HEAD_V), index)

    def h_index(i, j, r, pt):
        return jnp.where(j == 0, i * R + r, i * R + R - 1), 0

    def y_index(i, j, r, pt):
        return jnp.where(j == n_j - 1, i * R + r, i * R), 0

    vec = pl.BlockSpec((1, HEAD_QK), lambda i, j, r, pt: (0, 0))
    kern = functools.partial(_mlp_sample_kernel, pages_per_step=P, steps_per_seq=spp,
                             n_sample_steps=n_steps, page_size=page_size, ld=ld, past=past,
                             lam_init=lam_init)
    score_rows = N_HEADS * 2 * ld
    grid_spec = pltpu.PrefetchScalarGridSpec(
        num_scalar_prefetch=1,
        grid=(n_i, n_j, R),
        in_specs=[pl.BlockSpec((tm, d), h_index, pipeline_mode=pl.Buffered(1)),
                  pl.BlockSpec((1, d), lambda i, j, r, pt: (0, 0)),
                  pl.BlockSpec((d, tf), lambda i, j, r, pt: (0, j)),
                  pl.BlockSpec((tf, d), lambda i, j, r, pt: (j, 0)),
                  pl.BlockSpec((1, d), lambda i, j, r, pt: (0, 0)),
                  pl.BlockSpec(memory_space=pltpu.SMEM),
                  seq_spec(0), seq_spec(1), seq_spec(2)]
                 + [page_spec(k) for k in range(P)] + [page_spec(k) for k in range(P)]
                 + [vec, vec, vec, vec, pl.BlockSpec((1, HEAD_V), lambda i, j, r, pt: (0, 0))],
        out_specs=[pl.BlockSpec((tm, d), y_index),
                   pl.BlockSpec((ld, D_ATTN), lambda i, j, r, pt: (seq_and_group(i, j, r)[0], 0))],
        scratch_shapes=[pltpu.VMEM((R, tm, d), BF16),
                        pltpu.VMEM((R, tm, d), F32),
                        pltpu.VMEM((N_HEADS // 2, 4 * ld, HEAD_V), BF16),
                        pltpu.VMEM((score_rows, 1), F32),
                        pltpu.VMEM((score_rows, 1), F32),
                        pltpu.VMEM((score_rows, HEAD_V), F32),
                        pltpu.VMEM((score_rows, 2 * P * page_size), F32)])
    return pl.pallas_call(
        kern,
        grid_spec=grid_spec,
        out_shape=[jax.ShapeDtypeStruct((rows, d), F32),
                   jax.ShapeDtypeStruct((bd * ld, D_ATTN), F32)],
        compiler_params=_params(3),
        name="mlp_sample_attn",
    )(page_table.reshape(-1), h, g, w_up, w_down, g_final, slopes, z_s, z_s, z_s,
      *([k2] * P), *([v2] * P), *lam_vecs, subln_g)


def _mlp(h, g, w_up, w_down, g_final, tm=512, tf=1024):
    rows, d = h.shape
    d_ff = w_up.shape[1]
    tm = _row_tile(rows, tm)
    return pl.pallas_call(
        _mlp_kernel,
        grid=(rows // tm, d_ff // tf),
        in_specs=[pl.BlockSpec((tm, d), lambda i, j: (i, 0)),
                  pl.BlockSpec((1, d), lambda i, j: (0, 0)),
                  pl.BlockSpec((d, tf), lambda i, j: (0, j)),
                  pl.BlockSpec((tf, d), lambda i, j: (j, 0)),
                  pl.BlockSpec((1, d), lambda i, j: (0, 0))],
        out_specs=pl.BlockSpec((tm, d), lambda i, j: (i, 0)),
        out_shape=jax.ShapeDtypeStruct((rows, d), F32),
        scratch_shapes=[pltpu.VMEM((tm, d), BF16)],
        compiler_params=_params(2),
        name="mlp",
    )(h, g, w_up, w_down, g_final)


def kernel(x_prompt, x_sample, cache_k, cache_v, state_pool, page_table, meta_tokens, norm_mix_g,
           w_in, lambda_q1, lambda_k1, lambda_q2, lambda_k2, subln_g, w_pool, pool_scale, w_out,
           norm_mlp_g, w_up, w_down, norm_final_g):
    depth = norm_mix_g.shape[0]
    batch, seq, d_model = x_prompt.shape
    bd, ld, _ = x_sample.shape
    n_meta = meta_tokens.shape[0]
    assert depth == 1 and batch == 1, "one layer and one prompt sequence are supported"
    assert n_meta == POOL_HALO and (bd * ld) % n_meta == 0 and seq >= POOL_HALO
    layer = 0
    lam_init = 0.8 - 0.6 * math.exp(-0.3 * layer)
    slopes = 2.0 ** (-8.0 * jnp.arange(1, N_HEADS + 1, dtype=F32) / N_HEADS)
    n_small = bd * ld
    d_pool = w_pool.shape[1] * w_pool.shape[2]

    x_big = x_prompt.reshape(seq, d_model)
    x_small = jnp.concatenate([x_sample.reshape(n_small, d_model), meta_tokens.astype(F32)], axis=0)

    g_mix = norm_mix_g[layer].reshape(1, d_model)
    g_mlp = norm_mlp_g[layer].reshape(1, d_model)
    g_final = norm_final_g.reshape(1, d_model)
    g_sub = subln_g[layer].reshape(1, HEAD_V)
    scale_pool = pool_scale[layer].reshape(1, d_pool)
    lam_vecs = [v[layer].reshape(1, HEAD_QK) for v in (lambda_q1, lambda_k1, lambda_q2, lambda_k2)]
    w_in_b = w_in[layer].astype(BF16)
    w_out_b = w_out[layer].astype(BF16)
    w_up_b = w_up[layer].astype(BF16)
    w_down_b = w_down[layer].astype(BF16)
    w_pool_b = w_pool[layer].astype(BF16)

    z_b = _in_proj(x_big, g_mix, w_in_b)
    z_s = _in_proj(x_small, g_mix, w_in_b)

    a_b = _prompt_attention(z_b, z_s, n_small, slopes, lam_vecs, g_sub, lam_init, n_meta)
    a_meta = _meta_attention(z_s, n_small, slopes, lam_vecs, g_sub, lam_init, n_meta)

    u_b = z_b[:, 3 * D_ATTN:]
    u_s = z_s[:n_small, 3 * D_ATTN:].reshape(bd, ld, d_pool)
    u_meta = z_s[n_small:, 3 * D_ATTN:]
    hist = state_pool[layer].astype(F32)
    n_hist = hist.shape[1]
    zpad = jnp.zeros((bd, POOL_HALO - n_hist, d_pool), F32)
    ext = jnp.concatenate([jnp.concatenate([zpad, hist, u_s], axis=1).reshape(-1, d_pool),
                           jnp.zeros((POOL_HALO, d_pool), F32), u_meta], axis=0)
    m_b = _pool_big(z_b, z_s, n_small, w_pool_b, scale_pool)
    m_s = _pool_small(ext, w_pool_b, scale_pool, bd, ld, n_meta)

    h_b = _out_proj(x_big, a_b, m_b, w_out_b)
    y_b, a_samp = _mlp_with_sample_attention(h_b, g_mlp, w_up_b, w_down_b, g_final, z_s, cache_k[layer],
                                             cache_v[layer], page_table, slopes, lam_vecs, g_sub,
                                             lam_init, ld)
    a_s = jnp.concatenate([a_samp, a_meta], axis=0)
    h_s = _out_proj(x_small, a_s, m_s, w_out_b)
    y_s = _mlp(h_s, g_mlp, w_up_b, w_down_b, g_final)

    def heads(z, col):
        return z[:, col * D_ATTN:(col + 1) * D_ATTN].reshape(-1, N_HEADS, HEAD_V)

    t = seq + n_meta
    k_prompt = jnp.concatenate([heads(z_s[n_small:], 1), heads(z_b, 1)], axis=0)
    v_prompt = jnp.concatenate([heads(z_s[n_small:], 2), heads(z_b, 2)], axis=0)
    u_ext = jnp.concatenate([hist, u_s], axis=1)
    return (y_b.reshape(1, seq, d_model),
            y_s[:n_small].reshape(bd, ld, d_model),
            k_prompt.reshape(1, 1, t, N_HEADS, HEAD_V),
            v_prompt.reshape(1, 1, t, N_HEADS, HEAD_V),
            u_b[-n_hist:].reshape(1, 1, n_hist, d_pool),
            heads(z_s[:n_small], 1).reshape(1, bd, ld, N_HEADS, HEAD_V),
            heads(z_s[:n_small], 2).reshape(1, bd, ld, N_HEADS, HEAD_V),
            u_ext[:, -n_hist:].reshape(1, bd, n_hist, d_pool))
```

```python
import functools
import math

import jax
import jax.numpy as jnp
from jax import lax
from jax.experimental import pallas as pl
from jax.experimental.pallas import tpu as pltpu

N_HEADS = 8
HEAD_V = 128
HEAD_QK = HEAD_V // 2
D_ATTN = N_HEADS * HEAD_V
POOL_WINDOWS = (2, 4, 8, 16)
POOL_HALO = 16
EPS = 1e-6
MASK_VALUE = -1e30
QK_SCALE = HEAD_QK ** -0.5
LOG2E = 1.4426950408889634
POS_RADIX = 32
ACC_ROWS = HEAD_V + 16

VMEM_LIMIT_BYTES = 56 * 1024 * 1024
MAX_PAGES_PER_STEP = 16
MLP_ROW_TILES_PER_WEIGHT_FETCH = 2
BF16 = jnp.bfloat16
F32 = jnp.float32

_NT = (((1,), (1,)), ((), ()))


def _params(n_grid_dims):
    return pltpu.CompilerParams(dimension_semantics=("arbitrary",) * n_grid_dims,
                                vmem_limit_bytes=VMEM_LIMIT_BYTES)


def _row_tile(rows, target):
    best = rows
    for t in range(16, min(rows, target) + 1, 16):
        if rows % t == 0:
            best = t
    return best if best <= target else rows


def _rmsnorm(x, g):
    ms = jnp.mean(x * x, axis=-1, keepdims=True)
    return x * lax.rsqrt(ms + EPS) * g


def _in_proj_kernel(x_ref, g_ref, w_ref, z_ref, k_ref, v_ref, xn_ref, *, tn):
    j = pl.program_id(1)
    tm = x_ref.shape[0]

    @pl.when(j == 0)
    def _():
        xn_ref[...] = _rmsnorm(x_ref[...], g_ref[...]).astype(BF16)

    z = jnp.dot(xn_ref[...], w_ref[...], preferred_element_type=F32)
    z_ref[...] = z

    def per_head_rows(out_ref, first_col):
        blk, col = divmod(first_col, tn)

        @pl.when(j == blk)
        def _():
            for h in range(N_HEADS):
                out_ref[pl.ds(h, tm, stride=N_HEADS), :] = z[:, col + h * HEAD_V:col + (h + 1) * HEAD_V]

    per_head_rows(k_ref, D_ATTN)
    per_head_rows(v_ref, 2 * D_ATTN)


def _in_proj(x, g, w, tm_target=512, tn=2048):
    rows, d = x.shape
    n = w.shape[1]
    tm = _row_tile(rows, tm_target)
    assert tn % D_ATTN == 0
    per_head = jax.ShapeDtypeStruct((rows * N_HEADS, HEAD_V), F32)
    head_spec = pl.BlockSpec((tm * N_HEADS, HEAD_V), lambda i, j: (i, 0))
    z, k, v = pl.pallas_call(
        functools.partial(_in_proj_kernel, tn=tn),
        grid=(rows // tm, n // tn),
        in_specs=[pl.BlockSpec((tm, d), lambda i, j: (i, 0)),
                  pl.BlockSpec((1, d), lambda i, j: (0, 0)),
                  pl.BlockSpec((d, tn), lambda i, j: (0, j))],
        out_specs=[pl.BlockSpec((tm, tn), lambda i, j: (i, j)), head_spec, head_spec],
        out_shape=[jax.ShapeDtypeStruct((rows, n), F32), per_head, per_head],
        scratch_shapes=[pltpu.VMEM((tm, d), BF16)],
        compiler_params=_params(2),
        name="in_proj",
    )(x, g, w)
    return z, k.reshape(rows, N_HEADS, HEAD_V), v.reshape(rows, N_HEADS, HEAD_V)


def _stack_maps(q, scale=QK_SCALE):
    q = q * scale
    lane = lax.broadcasted_iota(jnp.int32, q.shape, 1)
    q1 = jnp.where(lane < HEAD_QK, q, 0.0)
    q2 = jnp.where(lane >= HEAD_QK, q, 0.0)
    return jnp.concatenate([q1, q2], axis=0).astype(BF16)


def _diff_lambda(lq1_ref, lk1_ref, lq2_ref, lk2_ref, lam_init):
    a = jnp.sum(lq1_ref[...] * lk1_ref[...], axis=-1, keepdims=True)
    b = jnp.sum(lq2_ref[...] * lk2_ref[...], axis=-1, keepdims=True)
    return jnp.exp(a) - jnp.exp(b) + lam_init


def _diff_combine(acc, l, lam, g, lam_init):
    n = acc.shape[0] // 2
    o = acc[:n] / l[:n] - lam * (acc[n:] / l[n:])
    return _rmsnorm(o, g) * (1.0 - lam_init)


def _prompt_attn_kernel(slope_ref, q_ref, k_ref, v_ref, km_ref, vm_ref,
                        lq1_ref, lk1_ref, lq2_ref, lk2_ref, gcol_ref, o_ref,
                        kb_ref, vt_ref, kmb_ref, vmt_ref, q2_ref, m_ref, acc_ref,
                        mask_ref, sa_ref, sb_ref, *, tq, n_meta, lam_init):
    h = pl.program_id(0)
    i = pl.program_id(1)
    slope = slope_ref[h] * LOG2E
    n_chunks = kb_ref.shape[0]

    @pl.when(i == 0)
    def _():
        key = lax.broadcasted_iota(jnp.int32, (tq, HEAD_V), 0)
        lane = lax.broadcasted_iota(jnp.int32, (tq, HEAD_V), 1)
        kfeat = jnp.where(lane < 3, lax.div(key, POS_RADIX),
                          jnp.where(lane < 6, lax.rem(key, POS_RADIX),
                                    jnp.where(lane < 9, 1, 0))).astype(BF16)
        sub = lax.broadcasted_iota(jnp.int32, (ACC_ROWS - HEAD_V, tq), 0)
        ones_row = jnp.where(sub == 0, 1.0, 0.0).astype(BF16)
        for c in range(n_chunks):
            kb_ref[c, :, :HEAD_V] = k_ref[c * tq:(c + 1) * tq, :].astype(BF16)
            kb_ref[c, :, HEAD_V:] = kfeat
            vt_ref[c, :HEAD_V, :] = v_ref[c * tq:(c + 1) * tq, :].T.astype(BF16)
            vt_ref[c, HEAD_V:, :] = ones_row
        pad = jnp.zeros((HEAD_V - n_meta, HEAD_V), F32)
        kmb_ref[:, :HEAD_V] = jnp.concatenate([km_ref[...], pad], axis=0).astype(BF16)
        kmb_ref[:, HEAD_V:] = jnp.zeros((HEAD_V, HEAD_V), BF16)
        vmt_ref[:HEAD_V, :] = jnp.concatenate([vm_ref[...], pad], axis=0).T.astype(BF16)
        vmt_ref[HEAD_V:, :] = ones_row[:, :HEAD_V]
        key = lax.broadcasted_iota(jnp.int32, (tq, 2 * tq), 0)
        qry = lax.broadcasted_iota(jnp.int32, (tq, 2 * tq), 1)
        qry = jnp.where(qry >= tq, qry - tq, qry)
        mask_ref[...] = jnp.where(key <= qry, 0.0, MASK_VALUE)
        qrow = lax.broadcasted_iota(jnp.int32, (2 * tq, HEAD_V), 0)
        lane = lax.broadcasted_iota(jnp.int32, (2 * tq, HEAD_V), 1)
        qrow = jnp.where(qrow >= tq, qrow - tq, qrow).astype(F32)
        whole = jnp.where(lane < 3, slope * POS_RADIX, jnp.where(lane < 6, slope, -slope * qrow))
        piece1 = whole.astype(BF16).astype(F32)
        rest = whole - piece1
        piece2 = rest.astype(BF16).astype(F32)
        piece3 = rest - piece2
        third = lax.rem(lane, 3)
        qfeat = jnp.where(third == 0, piece1, jnp.where(third == 1, piece2, piece3))
        q2_ref[:, HEAD_V:] = jnp.where(lane < 9, qfeat, 0.0).astype(BF16)

    q2_ref[:, :HEAD_V] = _stack_maps(q_ref[...], QK_SCALE * LOG2E)

    def scores(c, s_ref):
        s_ref[...] = lax.dot_general(kb_ref[c], q2_ref[...], _NT, preferred_element_type=F32)

    scores(0, sa_ref)

    key = lax.broadcasted_iota(jnp.int32, (HEAD_V, 2 * tq), 0)
    qry = lax.broadcasted_iota(jnp.int32, (HEAD_V, 2 * tq), 1)
    qry = jnp.where(qry >= tq, qry - tq, qry)
    qpos = n_meta + i * tq + qry
    s = lax.dot_general(kmb_ref[...], q2_ref[...], _NT, preferred_element_type=F32)
    s = jnp.where(key < n_meta, s + slope * (key - qpos).astype(F32), MASK_VALUE)
    m0 = jnp.max(s, axis=0, keepdims=True)
    m_ref[...] = m0
    acc_ref[...] = jnp.dot(vmt_ref[...], jnp.exp2(s - m0).astype(BF16), preferred_element_type=F32)

    def softmax_update(c, s_ref, diagonal):
        s = s_ref[...]
        if diagonal:
            s = s + mask_ref[...]
        sigma = slope * (tq * (c - i)).astype(F32)
        m_old = m_ref[...]
        m_new = jnp.maximum(m_old, jnp.max(s, axis=0, keepdims=True) + sigma)
        p = jnp.exp2(s - (m_new - sigma)).astype(BF16)
        alpha = jnp.exp2(m_old - m_new)
        acc_ref[...] = alpha * acc_ref[...] + jnp.dot(vt_ref[c], p, preferred_element_type=F32)
        m_ref[...] = m_new

    def pair(j, carry):
        c = 2 * j
        scores(c + 1, sb_ref)
        softmax_update(c, sa_ref, False)
        scores(c + 2, sa_ref)
        softmax_update(c + 1, sb_ref, False)
        return carry

    lax.fori_loop(0, lax.shift_right_logical(i, 1), pair, 0)

    @pl.when((i & 1) == 0)
    def _():
        softmax_update(i, sa_ref, True)

    @pl.when((i & 1) == 1)
    def _():
        scores(i, sb_ref)
        softmax_update(i - 1, sa_ref, False)
        softmax_update(i, sb_ref, True)

    lam = _diff_lambda(lq1_ref, lk1_ref, lq2_ref, lk2_ref, lam_init)
    acc = acc_ref[:HEAD_V, :]
    l = acc_ref[HEAD_V:HEAD_V + 1, :]
    o = acc[:, :tq] / l[:, :tq] - lam * (acc[:, tq:] / l[:, tq:])
    ms = jnp.mean(o * o, axis=0, keepdims=True)
    o = o * lax.rsqrt(ms + EPS) * gcol_ref[...] * (1.0 - lam_init)
    o_ref[...] = o.T.astype(o_ref.dtype)


def _prompt_attention(z_b, z_s, meta_row0, slopes, lam_vecs, subln_g, lam_init, n_meta, tq=512):
    seq = z_b.shape[0]
    tq = _row_tile(seq, tq)
    assert tq % 128 == 0 and meta_row0 % n_meta == 0 and n_meta <= HEAD_V
    assert tq <= POS_RADIX * POS_RADIX, "key index digits must be exact in bf16"
    meta_blk = meta_row0 // n_meta
    vec = pl.BlockSpec((1, HEAD_QK), lambda h, i: (0, 0))
    kern = functools.partial(_prompt_attn_kernel, tq=tq, n_meta=n_meta, lam_init=lam_init)
    return pl.pallas_call(
        kern,
        grid=(N_HEADS, seq // tq),
        in_specs=[pl.BlockSpec(memory_space=pltpu.SMEM),
                  pl.BlockSpec((tq, HEAD_V), lambda h, i: (i, h)),
                  pl.BlockSpec((seq, HEAD_V), lambda h, i: (0, N_HEADS + h)),
                  pl.BlockSpec((seq, HEAD_V), lambda h, i: (0, 2 * N_HEADS + h)),
                  pl.BlockSpec((n_meta, HEAD_V), lambda h, i: (meta_blk, N_HEADS + h)),
                  pl.BlockSpec((n_meta, HEAD_V), lambda h, i: (meta_blk, 2 * N_HEADS + h)),
                  vec, vec, vec, vec,
                  pl.BlockSpec((HEAD_V, 1), lambda h, i: (0, 0))],
        out_specs=pl.BlockSpec((tq, HEAD_V), lambda h, i: (i, h)),
        out_shape=jax.ShapeDtypeStruct((seq, D_ATTN), BF16),
        scratch_shapes=[pltpu.VMEM((seq // tq, tq, 2 * HEAD_V), BF16),
                        pltpu.VMEM((seq // tq, ACC_ROWS, tq), BF16),
                        pltpu.VMEM((HEAD_V, 2 * HEAD_V), BF16),
                        pltpu.VMEM((ACC_ROWS, HEAD_V), BF16),
                        pltpu.VMEM((2 * tq, 2 * HEAD_V), BF16),
                        pltpu.VMEM((1, 2 * tq), F32),
                        pltpu.VMEM((ACC_ROWS, 2 * tq), F32),
                        pltpu.VMEM((tq, 2 * tq), F32),
                        pltpu.VMEM((tq, 2 * tq), F32),
                        pltpu.VMEM((tq, 2 * tq), F32)],
        compiler_params=_params(2),
        name="prompt_attn",
    )(slopes, z_b, z_b, z_b, z_s, z_s, *lam_vecs, subln_g.reshape(HEAD_V, 1))


def _meta_attn_kernel(slope_ref, q_ref, k_ref, v_ref, lq1_ref, lk1_ref, lq2_ref, lk2_ref, g_ref,
                      o_ref, *, n_meta, lam_init):
    slope = slope_ref[pl.program_id(0)]
    q2 = _stack_maps(q_ref[...])
    row = lax.broadcasted_iota(jnp.int32, (2 * n_meta, n_meta), 0)
    col = lax.broadcasted_iota(jnp.int32, (2 * n_meta, n_meta), 1)
    row = jnp.where(row >= n_meta, row - n_meta, row)
    s = lax.dot_general(q2, k_ref[...].astype(BF16), _NT, preferred_element_type=F32)
    s = jnp.where(col <= row, s + slope * (col - row).astype(F32), MASK_VALUE)
    p = jnp.exp(s - jnp.max(s, axis=-1, keepdims=True))
    l = jnp.sum(p, axis=-1, keepdims=True)
    acc = jnp.dot(p.astype(BF16), v_ref[...].astype(BF16), preferred_element_type=F32)
    lam = _diff_lambda(lq1_ref, lk1_ref, lq2_ref, lk2_ref, lam_init)
    o_ref[...] = _diff_combine(acc, l, lam, g_ref[...], lam_init)


def _meta_attention(z_s, meta_row0, slopes, lam_vecs, subln_g, lam_init, n_meta):
    meta_blk = meta_row0 // n_meta
    vec = pl.BlockSpec((1, HEAD_QK), lambda h: (0, 0))
    kern = functools.partial(_meta_attn_kernel, n_meta=n_meta, lam_init=lam_init)
    return pl.pallas_call(
        kern,
        grid=(N_HEADS,),
        in_specs=[pl.BlockSpec(memory_space=pltpu.SMEM),
                  pl.BlockSpec((n_meta, HEAD_V), lambda h: (meta_blk, h)),
                  pl.BlockSpec((n_meta, HEAD_V), lambda h: (meta_blk, N_HEADS + h)),
                  pl.BlockSpec((n_meta, HEAD_V), lambda h: (meta_blk, 2 * N_HEADS + h)),
                  vec, vec, vec, vec,
                  pl.BlockSpec((1, HEAD_V), lambda h: (0, 0))],
        out_specs=pl.BlockSpec((n_meta, HEAD_V), lambda h: (0, h)),
        out_shape=jax.ShapeDtypeStruct((n_meta, D_ATTN), F32),
        compiler_params=_params(1),
        name="meta_attn",
    )(slopes, z_s, z_s, z_s, *lam_vecs, subln_g)


def _sample_attn_parts(slope_ref, q_ref, kn_ref, vn_ref, k_pages, v_pages, lam_refs, g_ref, o_ref,
                       q2_ref, m_ref, l_ref, acc_ref, bias_ref, *, page_size, ld, past, lam_init):
    lq1_ref, lk1_ref, lq2_ref, lk2_ref = lam_refs
    tk = len(k_pages) * page_size
    n_pairs = N_HEADS // 2
    hr = 2 * ld
    pr = 2 * hr
    rows = N_HEADS * hr
    pair_heads = [(j, j + n_pairs) for j in range(n_pairs)]

    slope_col = jnp.concatenate([jnp.full((hr, 1), slope_ref[h], F32)
                                 for pair in pair_heads for h in pair], axis=0)

    def row_head_and_query(shape):
        row = lax.broadcasted_iota(jnp.int32, shape, 0)
        return lax.rem(lax.div(row, hr), 2), lax.rem(row, ld)

    def prologue(first_step, seq_start):
        @pl.when(first_step)
        def _():
            rh, qi = row_head_and_query((rows, 2 * tk))
            col = lax.broadcasted_iota(jnp.int32, (rows, 2 * tk), 1)
            rel = (lax.div(col, 2) - qi).astype(F32)
            bias_ref[...] = jnp.where(lax.rem(col, 2) == rh, slope_col * rel, MASK_VALUE)

        @pl.when(seq_start)
        def _():
            for j, pair in enumerate(pair_heads):
                q2_ref[j] = jnp.concatenate(
                    [_stack_maps(q_ref[:, h * HEAD_V:(h + 1) * HEAD_V]) for h in pair], axis=0)
            m_ref[...] = jnp.full(m_ref.shape, MASK_VALUE, F32)
            l_ref[...] = jnp.zeros(l_ref.shape, F32)
            acc_ref[...] = jnp.zeros(acc_ref.shape, F32)

    def update(s, values):
        m_old = m_ref[...]
        m_new = jnp.maximum(m_old, jnp.max(s, axis=-1, keepdims=True))
        p = jnp.exp(s - m_new)
        alpha = jnp.exp(m_old - m_new)
        l_ref[...] = alpha * l_ref[...] + jnp.sum(p, axis=-1, keepdims=True)
        p = p.astype(BF16)
        pv = [jnp.dot(p[j * pr:(j + 1) * pr], values[j], preferred_element_type=F32)
              for j in range(n_pairs)]
        acc_ref[...] = alpha * acc_ref[...] + jnp.concatenate(pv, axis=0)
        m_ref[...] = m_new

    def pair_rows(pages, j):
        return jnp.concatenate([pg[0, pl.ds(j, 2 * page_size, stride=n_pairs), :] for pg in pages],
                               axis=0).astype(BF16)

    def main(c, between=None):
        s_parts = [lax.dot_general(q2_ref[j], pair_rows(k_pages, j), _NT, preferred_element_type=F32)
                   for j in range(n_pairs)]
        if between is not None:
            between()
        values = [pair_rows(v_pages, j) for j in range(n_pairs)]
        group_offset = slope_col * (c * tk - past).astype(F32)
        update(jnp.concatenate(s_parts, axis=0) + bias_ref[...] + group_offset, values)

    def epilogue(seq_end):
        @pl.when(seq_end)
        def _():
            lam = _diff_lambda(lq1_ref, lk1_ref, lq2_ref, lk2_ref, lam_init)
            rh, qi = row_head_and_query((rows, hr))
            col = lax.broadcasted_iota(jnp.int32, (rows, hr), 1)
            kj = lax.rem(col, ld)
            visible = (lax.div(col, ld) == rh) & (kj <= qi)
            s_parts, values = [], []
            for j, pair in enumerate(pair_heads):
                kn = jnp.concatenate([kn_ref[:, h * HEAD_V:(h + 1) * HEAD_V] for h in pair], axis=0)
                vn = jnp.concatenate([vn_ref[:, h * HEAD_V:(h + 1) * HEAD_V] for h in pair], axis=0)
                s_parts.append(lax.dot_general(q2_ref[j], kn.astype(BF16), _NT,
                                               preferred_element_type=F32))
                values.append(vn.astype(BF16))
            s = jnp.concatenate(s_parts, axis=0) + slope_col * (kj - qi).astype(F32)
            update(jnp.where(visible, s, MASK_VALUE), values)
            acc = acc_ref[...]
            l = l_ref[...]
            for j, pair in enumerate(pair_heads):
                for t, h in enumerate(pair):
                    r0 = j * pr + t * hr
                    o_ref[:, h * HEAD_V:(h + 1) * HEAD_V] = _diff_combine(
                        acc[r0:r0 + hr], l[r0:r0 + hr], lam, g_ref[...], lam_init)

    return prologue, main, epilogue


def _window_sums(ext):
    out = {}
    s = ext
    w = 1
    while w < max(POOL_WINDOWS):
        s = s + pltpu.roll(s, w, 0)
        w *= 2
        out[w] = s
    return out


def _pool_project(u, sums, inv_cnt, wp_ref, scale_ref):
    cg = u.shape[1] // len(POOL_WINDOWS)
    outs = []
    for g, w in enumerate(POOL_WINDOWS):
        sl = slice(g * cg, (g + 1) * cg)
        d = sums[w][:, sl] * inv_cnt[w] - u[:, sl]
        outs.append(jnp.dot(d.astype(BF16), wp_ref[g], preferred_element_type=F32))
    return jnp.concatenate(outs, axis=1) * scale_ref[...]


def _pool_big_kernel(u_ref, prev_ref, meta_ref, wp_ref, scale_ref, m_ref):
    halo = jnp.where(pl.program_id(0) == 0, meta_ref[...], prev_ref[...])
    u = u_ref[...]
    sums = _window_sums(jnp.concatenate([halo, u], axis=0))
    sums = {w: s[POOL_HALO:] for w, s in sums.items()}
    inv = {w: 1.0 / w for w in POOL_WINDOWS}
    m_ref[...] = _pool_project(u, sums, inv, wp_ref, scale_ref).astype(m_ref.dtype)


def _pool_big(z_b, z_s, meta_row0, w_pool, pool_scale, tm=512):
    seq = z_b.shape[0]
    d_pool = w_pool.shape[0] * w_pool.shape[1]
    tm = _row_tile(seq, tm)
    ucol = z_b.shape[1] // d_pool - 1
    per = tm // POOL_HALO
    return pl.pallas_call(
        _pool_big_kernel,
        grid=(seq // tm,),
        in_specs=[pl.BlockSpec((tm, d_pool), lambda i: (i, ucol)),
                  pl.BlockSpec((POOL_HALO, d_pool), lambda i: (jnp.maximum(i * per - 1, 0), ucol)),
                  pl.BlockSpec((POOL_HALO, d_pool), lambda i: (meta_row0 // POOL_HALO, ucol)),
                  pl.BlockSpec(w_pool.shape, lambda i: (0, 0, 0)),
                  pl.BlockSpec((1, d_pool), lambda i: (0, 0))],
        out_specs=pl.BlockSpec((tm, d_pool), lambda i: (i, 0)),
        out_shape=jax.ShapeDtypeStruct((seq, d_pool), BF16),
        compiler_params=_params(1),
        name="pool_big",
    )(z_b, z_b, z_s, w_pool, pool_scale)


def _pool_small_kernel(ext_ref, wp_ref, scale_ref, m_ref, *, bd, ld, n_meta):
    grp = POOL_HALO + ld
    ext = ext_ref[...]
    sums = _window_sums(ext)
    meta0 = bd * grp + POOL_HALO

    def new_rows(a):
        parts = [a[b * grp + POOL_HALO:(b + 1) * grp] for b in range(bd)]
        return jnp.concatenate(parts + [a[meta0:meta0 + n_meta]], axis=0)

    n = bd * ld + n_meta
    r = lax.broadcasted_iota(jnp.int32, (n, 1), 0)
    inv = {}
    for w in POOL_WINDOWS:
        cnt = jnp.where(r < bd * ld, w, jnp.minimum(w, r - bd * ld + 1))
        inv[w] = 1.0 / cnt.astype(F32)
    sums = {w: new_rows(s) for w, s in sums.items()}
    m_ref[...] = _pool_project(new_rows(ext), sums, inv, wp_ref, scale_ref).astype(m_ref.dtype)


def _pool_small(ext, w_pool, pool_scale, bd, ld, n_meta):
    n = bd * ld + n_meta
    d_pool = ext.shape[1]
    kern = functools.partial(_pool_small_kernel, bd=bd, ld=ld, n_meta=n_meta)
    return pl.pallas_call(
        kern,
        grid=(1,),
        in_specs=[pl.BlockSpec(ext.shape, lambda i: (0, 0)),
                  pl.BlockSpec(w_pool.shape, lambda i: (0, 0, 0)),
                  pl.BlockSpec((1, d_pool), lambda i: (0, 0))],
        out_specs=pl.BlockSpec((n, d_pool), lambda i: (0, 0)),
        out_shape=jax.ShapeDtypeStruct((n, d_pool), BF16),
        compiler_params=_params(1),
        name="pool_small",
    )(ext, w_pool, pool_scale)


def _out_proj_kernel(x_ref, a_ref, m_ref, wa_ref, wm_ref, h_ref):
    h_ref[...] = (x_ref[...]
                  + jnp.dot(a_ref[...].astype(BF16), wa_ref[...], preferred_element_type=F32)
                  + jnp.dot(m_ref[...], wm_ref[...], preferred_element_type=F32))


def _out_proj(x, a, m, w_out, tm=512):
    rows, d = x.shape
    da, dm = a.shape[1], m.shape[1]
    tm = _row_tile(rows, tm)
    return pl.pallas_call(
        _out_proj_kernel,
        grid=(rows // tm,),
        in_specs=[pl.BlockSpec((tm, d), lambda i: (i, 0)),
                  pl.BlockSpec((tm, da), lambda i: (i, 0)),
                  pl.BlockSpec((tm, dm), lambda i: (i, 0)),
                  pl.BlockSpec((da, d), lambda i: (0, 0)),
                  pl.BlockSpec((dm, d), lambda i: (da // dm, 0))],
        out_specs=pl.BlockSpec((tm, d), lambda i: (i, 0)),
        out_shape=jax.ShapeDtypeStruct((rows, d), F32),
        compiler_params=_params(1),
        name="out_proj",
    )(x, a, m, w_out, w_out)


def _mlp_parts(h_ref, g_ref, wu_ref, wd_ref, gf_ref, y_ref, xn_ref, acc_ref):
    j = pl.program_id(1)

    def prologue():
        @pl.when(j == 0)
        def _():
            h = h_ref[...]
            xn_ref[...] = _rmsnorm(h, g_ref[...]).astype(BF16)
            acc_ref[...] = h

    def main():
        a = jnp.maximum(jnp.dot(xn_ref[...], wu_ref[...], preferred_element_type=F32), 0.0)
        acc_ref[...] += jnp.dot((a * a).astype(BF16), wd_ref[...], preferred_element_type=F32)

    def epilogue():
        @pl.when(j == pl.num_programs(1) - 1)
        def _():
            y_ref[...] = _rmsnorm(acc_ref[...], gf_ref[...])

    return prologue, main, epilogue


def _mlp_kernel(h_ref, g_ref, wu_ref, wd_ref, gf_ref, y_ref, xn_ref):
    prologue, main, epilogue = _mlp_parts(h_ref, g_ref, wu_ref, wd_ref, gf_ref, y_ref, xn_ref, y_ref)
    prologue()
    main()
    epilogue()


def _mlp_sample_kernel(pt_ref, h_ref, g_ref, wu_ref, wd_ref, gf_ref, slope_ref, q_ref, kn_ref, vn_ref,
                       *rest, pages_per_step, steps_per_seq, n_sample_steps, page_size, ld, past,
                       lam_init):
    del pt_ref
    P = pages_per_step
    k_pages, v_pages = rest[:P], rest[P:2 * P]
    lam_refs = rest[2 * P:2 * P + 4]
    (g_sub_ref, y_ref, o_ref, xn_ref, yacc_ref,
     q2_ref, m_ref, l_ref, acc_ref, bias_ref) = rest[2 * P + 4:]
    r = pl.program_id(2)
    t = (pl.program_id(0) * pl.num_programs(1) + pl.program_id(1)) * pl.num_programs(2) + r
    active = t < n_sample_steps
    c = lax.rem(jnp.minimum(t, n_sample_steps - 1), steps_per_seq)
    mlp_prologue, mlp_main, mlp_epilogue = _mlp_parts(h_ref, g_ref, wu_ref, wd_ref, gf_ref, y_ref,
                                                      xn_ref.at[r], yacc_ref.at[r])
    att_prologue, att_main, att_epilogue = _sample_attn_parts(
        slope_ref, q_ref, kn_ref, vn_ref, k_pages, v_pages, lam_refs, g_sub_ref, o_ref,
        q2_ref, m_ref, l_ref, acc_ref, bias_ref, page_size=page_size, ld=ld, past=past,
        lam_init=lam_init)
    mlp_prologue()
    att_prologue(t == 0, active & (c == 0))
    att_main(c, mlp_main)
    mlp_epilogue()
    att_epilogue(active & (c == steps_per_seq - 1))


def _mlp_with_sample_attention(h, g, w_up, w_down, g_final, z_s, cache_k, cache_v, page_table, slopes,
                               lam_vecs, subln_g, lam_init, ld, tm=512, tf=512):
    rows, d = h.shape
    d_ff = w_up.shape[1]
    tm = _row_tile(rows, tm)
    R = MLP_ROW_TILES_PER_WEIGHT_FETCH if (rows // tm) % MLP_ROW_TILES_PER_WEIGHT_FETCH == 0 else 1
    n_i, n_j = rows // (tm * R), d_ff // tf
    bd, n_pages = page_table.shape
    n_phys, page_size = cache_k.shape[0], cache_k.shape[1]
    past = n_pages * page_size
    P = min(p for p in range(1, n_pages + 1)
            if n_pages % p == 0 and bd * (n_pages // p) <= n_i * n_j * R)
    assert P <= MAX_PAGES_PER_STEP
    spp = n_pages // P
    n_steps = bd * spp
    k2 = cache_k.reshape(n_phys, page_size * N_HEADS, HEAD_V)
    v2 = cache_v.reshape(n_phys, page_size * N_HEADS, HEAD_V)

    def seq_and_group(i, j, r):
        t = jnp.minimum((i * n_j + j) * R + r, n_steps - 1)
        return t // spp, t % spp

    def seq_spec(col):
        return pl.BlockSpec((ld, D_ATTN), lambda i, j, r, pt: (seq_and_group(i, j, r)[0], col))

    def page_spec(k):
        def index(i, j, r, pt):
            b, c = seq_and_group(i, j, r)
            return pt[b * n_pages + c * P + k], 0, 0
        return pl.BlockSpec((1, page_size * N_HEADS, HEAD_V), index)

    def h_index(i, j, r, pt):
        return jnp.where(j == 0, i * R + r, i * R + R - 1), 0

    def y_index(i, j, r, pt):
        return jnp.where(j == n_j - 1, i * R + r, i * R), 0

    vec = pl.BlockSpec((1, HEAD_QK), lambda i, j, r, pt: (0, 0))
    kern = functools.partial(_mlp_sample_kernel, pages_per_step=P, steps_per_seq=spp,
                             n_sample_steps=n_steps, page_size=page_size, ld=ld, past=past,
                             lam_init=lam_init)
    score_rows = N_HEADS * 2 * ld
    grid_spec = pltpu.PrefetchScalarGridSpec(
        num_scalar_prefetch=1,
        grid=(n_i, n_j, R),
        in_specs=[pl.BlockSpec((tm, d), h_index, pipeline_mode=pl.Buffered(1)),
                  pl.BlockSpec((1, d), lambda i, j, r, pt: (0, 0)),
                  pl.BlockSpec((d, tf), lambda i, j, r, pt: (0, j)),
                  pl.BlockSpec((tf, d), lambda i, j, r, pt: (j, 0)),
                  pl.BlockSpec((1, d), lambda i, j, r, pt: (0, 0)),
                  pl.BlockSpec(memory_space=pltpu.SMEM),
                  seq_spec(0), seq_spec(1), seq_spec(2)]
                 + [page_spec(k) for k in range(P)] + [page_spec(k) for k in range(P)]
                 + [vec, vec, vec, vec, pl.BlockSpec((1, HEAD_V), lambda i, j, r, pt: (0, 0))],
        out_specs=[pl.BlockSpec((tm, d), y_index),
                   pl.BlockSpec((ld, D_ATTN), lambda i, j, r, pt: (seq_and_group(i, j, r)[0], 0))],
        scratch_shapes=[pltpu.VMEM((R, tm, d), BF16),
                        pltpu.VMEM((R, tm, d), F32),
                        pltpu.VMEM((N_HEADS // 2, 4 * ld, HEAD_V), BF16),
                        pltpu.VMEM((score_rows, 1), F32),
                        pltpu.VMEM((score_rows, 1), F32),
                        pltpu.VMEM((score_rows, HEAD_V), F32),
                        pltpu.VMEM((score_rows, 2 * P * page_size), F32)])
    return pl.pallas_call(
        kern,
        grid_spec=grid_spec,
        out_shape=[jax.ShapeDtypeStruct((rows, d), F32),
                   jax.ShapeDtypeStruct((bd * ld, D_ATTN), F32)],
        compiler_params=_params(3),
        name="mlp_sample_attn",
    )(page_table.reshape(-1), h, g, w_up, w_down, g_final, slopes, z_s, z_s, z_s,
      *([k2] * P), *([v2] * P), *lam_vecs, subln_g)


def _mlp(h, g, w_up, w_down, g_final, tm=512, tf=1024):
    rows, d = h.shape
    d_ff = w_up.shape[1]
    tm = _row_tile(rows, tm)
    return pl.pallas_call(
        _mlp_kernel,
        grid=(rows // tm, d_ff // tf),
        in_specs=[pl.BlockSpec((tm, d), lambda i, j: (i, 0)),
                  pl.BlockSpec((1, d), lambda i, j: (0, 0)),
                  pl.BlockSpec((d, tf), lambda i, j: (0, j)),
                  pl.BlockSpec((tf, d), lambda i, j: (j, 0)),
                  pl.BlockSpec((1, d), lambda i, j: (0, 0))],
        out_specs=pl.BlockSpec((tm, d), lambda i, j: (i, 0)),
        out_shape=jax.ShapeDtypeStruct((rows, d), F32),
        scratch_shapes=[pltpu.VMEM((tm, d), BF16)],
        compiler_params=_params(2),
        name="mlp",
    )(h, g, w_up, w_down, g_final)


def kernel(x_prompt, x_sample, cache_k, cache_v, state_pool, page_table, meta_tokens, norm_mix_g,
           w_in, lambda_q1, lambda_k1, lambda_q2, lambda_k2, subln_g, w_pool, pool_scale, w_out,
           norm_mlp_g, w_up, w_down, norm_final_g):
    depth = norm_mix_g.shape[0]
    batch, seq, d_model = x_prompt.shape
    bd, ld, _ = x_sample.shape
    n_meta = meta_tokens.shape[0]
    assert depth == 1 and batch == 1, "one layer and one prompt sequence are supported"
    assert n_meta == POOL_HALO and (bd * ld) % n_meta == 0 and seq >= POOL_HALO
    layer = 0
    lam_init = 0.8 - 0.6 * math.exp(-0.3 * layer)
    slopes = 2.0 ** (-8.0 * jnp.arange(1, N_HEADS + 1, dtype=F32) / N_HEADS)
    n_small = bd * ld
    d_pool = w_pool.shape[1] * w_pool.shape[2]

    x_big = x_prompt.reshape(seq, d_model)
    x_small = jnp.concatenate([x_sample.reshape(n_small, d_model), meta_tokens.astype(F32)], axis=0)

    g_mix = norm_mix_g[layer].reshape(1, d_model)
    g_mlp = norm_mlp_g[layer].reshape(1, d_model)
    g_final = norm_final_g.reshape(1, d_model)
    g_sub = subln_g[layer].reshape(1, HEAD_V)
    scale_pool = pool_scale[layer].reshape(1, d_pool)
    lam_vecs = [v[layer].reshape(1, HEAD_QK) for v in (lambda_q1, lambda_k1, lambda_q2, lambda_k2)]
    w_in_b = w_in[layer].astype(BF16)
    w_out_b = w_out[layer].astype(BF16)
    w_up_b = w_up[layer].astype(BF16)
    w_down_b = w_down[layer].astype(BF16)
    w_pool_b = w_pool[layer].astype(BF16)

    z_b, k_b, v_b = _in_proj(x_big, g_mix, w_in_b)
    z_s, k_s, v_s = _in_proj(x_small, g_mix, w_in_b)

    a_b = _prompt_attention(z_b, z_s, n_small, slopes, lam_vecs, g_sub, lam_init, n_meta)
    a_meta = _meta_attention(z_s, n_small, slopes, lam_vecs, g_sub, lam_init, n_meta)

    u_b = z_b[:, 3 * D_ATTN:]
    u_s = z_s[:n_small, 3 * D_ATTN:].reshape(bd, ld, d_pool)
    u_meta = z_s[n_small:, 3 * D_ATTN:]
    hist = state_pool[layer].astype(F32)
    n_hist = hist.shape[1]
    zpad = jnp.zeros((bd, POOL_HALO - n_hist, d_pool), F32)
    ext = jnp.concatenate([jnp.concatenate([zpad, hist, u_s], axis=1).reshape(-1, d_pool),
                           jnp.zeros((POOL_HALO, d_pool), F32), u_meta], axis=0)
    m_b = _pool_big(z_b, z_s, n_small, w_pool_b, scale_pool)
    m_s = _pool_small(ext, w_pool_b, scale_pool, bd, ld, n_meta)

    h_b = _out_proj(x_big, a_b, m_b, w_out_b)
    y_b, a_samp = _mlp_with_sample_attention(h_b, g_mlp, w_up_b, w_down_b, g_final, z_s, cache_k[layer],
                                             cache_v[layer], page_table, slopes, lam_vecs, g_sub,
                                             lam_init, ld)
    a_s = jnp.concatenate([a_samp, a_meta], axis=0)
    h_s = _out_proj(x_small, a_s, m_s, w_out_b)
    y_s = _mlp(h_s, g_mlp, w_up_b, w_down_b, g_final)

    t = seq + n_meta
    k_prompt = jnp.concatenate([k_s[n_small:], k_b], axis=0)
    v_prompt = jnp.concatenate([v_s[n_small:], v_b], axis=0)
    u_ext = jnp.concatenate([hist, u_s], axis=1)
    return (y_b.reshape(1, seq, d_model),
            y_s[:n_small].reshape(bd, ld, d_model),
            k_prompt.reshape(1, 1, t, N_HEADS, HEAD_V),
            v_prompt.reshape(1, 1, t, N_HEADS, HEAD_V),
            u_b[-n_hist:].reshape(1, 1, n_hist, d_pool),
            k_s[:n_small].reshape(1, bd, ld, N_HEADS, HEAD_V),
            v_s[:n_small].reshape(1, bd, ld, N_HEADS, HEAD_V),
            u_ext[:, -n_hist:].reshape(1, bd, n_hist, d_pool))
```

```python
import functools
import math

import jax
import jax.numpy as jnp
from jax import lax
from jax.experimental import pallas as pl
from jax.experimental.pallas import tpu as pltpu

N_HEADS = 8
HEAD_V = 128
HEAD_QK = HEAD_V // 2
D_ATTN = N_HEADS * HEAD_V
POOL_WINDOWS = (2, 4, 8, 16)
POOL_HALO = 16
EPS = 1e-6
MASK_VALUE = -1e30
QK_SCALE = HEAD_QK ** -0.5
LOG2E = 1.4426950408889634
POS_RADIX = 32
ACC_ROWS = HEAD_V + 16
CHUNKS_PER_LOOP_STEP = 4
assert CHUNKS_PER_LOOP_STEP % 2 == 0

VMEM_LIMIT_BYTES = 56 * 1024 * 1024
MAX_PAGES_PER_STEP = 16
MLP_ROW_TILES_PER_WEIGHT_FETCH = 2
BF16 = jnp.bfloat16
F32 = jnp.float32

_NT = (((1,), (1,)), ((), ()))


def _params(n_grid_dims):
    return pltpu.CompilerParams(dimension_semantics=("arbitrary",) * n_grid_dims,
                                vmem_limit_bytes=VMEM_LIMIT_BYTES)


def _row_tile(rows, target):
    best = rows
    for t in range(16, min(rows, target) + 1, 16):
        if rows % t == 0:
            best = t
    return best if best <= target else rows


def _rmsnorm(x, g):
    ms = jnp.mean(x * x, axis=-1, keepdims=True)
    return x * lax.rsqrt(ms + EPS) * g


def _in_proj_kernel(x_ref, g_ref, w_ref, z_ref, k_ref, v_ref, *, n_parts):
    tm = x_ref.shape[0]
    part = tm // n_parts
    for s in range(n_parts):
        rows = slice(s * part, (s + 1) * part)
        xn = _rmsnorm(x_ref[rows, :], g_ref[...]).astype(BF16)
        z = jnp.dot(xn, w_ref[...], preferred_element_type=F32)
        z_ref[rows, :] = z
        for out_ref, col in ((k_ref, D_ATTN), (v_ref, 2 * D_ATTN)):
            for h in range(N_HEADS):
                out_ref[pl.ds(s * part * N_HEADS + h, part, stride=N_HEADS), :] = (
                    z[:, col + h * HEAD_V:col + (h + 1) * HEAD_V])


def _in_proj(x, g, w, tm_target=512, n_parts=2):
    rows, d = x.shape
    n = w.shape[1]
    tm = _row_tile(rows, tm_target)
    if (tm // n_parts) % 16:
        n_parts = 1
    per_head = jax.ShapeDtypeStruct((rows * N_HEADS, HEAD_V), F32)
    head_spec = pl.BlockSpec((tm * N_HEADS, HEAD_V), lambda i: (i, 0))
    z, k, v = pl.pallas_call(
        functools.partial(_in_proj_kernel, n_parts=n_parts),
        grid=(rows // tm,),
        in_specs=[pl.BlockSpec((tm, d), lambda i: (i, 0)),
                  pl.BlockSpec((1, d), lambda i: (0, 0)),
                  pl.BlockSpec((d, n), lambda i: (0, 0), pipeline_mode=pl.Buffered(1))],
        out_specs=[pl.BlockSpec((tm, n), lambda i: (i, 0)), head_spec, head_spec],
        out_shape=[jax.ShapeDtypeStruct((rows, n), F32), per_head, per_head],
        compiler_params=_params(1),
        name="in_proj",
    )(x, g, w)
    return z, k.reshape(rows, N_HEADS, HEAD_V), v.reshape(rows, N_HEADS, HEAD_V)


def _stack_maps(q, scale=QK_SCALE):
    q = q * scale
    lane = lax.broadcasted_iota(jnp.int32, q.shape, 1)
    q1 = jnp.where(lane < HEAD_QK, q, 0.0)
    q2 = jnp.where(lane >= HEAD_QK, q, 0.0)
    return jnp.concatenate([q1, q2], axis=0).astype(BF16)


def _diff_lambda(lq1_ref, lk1_ref, lq2_ref, lk2_ref, lam_init):
    a = jnp.sum(lq1_ref[...] * lk1_ref[...], axis=-1, keepdims=True)
    b = jnp.sum(lq2_ref[...] * lk2_ref[...], axis=-1, keepdims=True)
    return jnp.exp(a) - jnp.exp(b) + lam_init


def _diff_combine(acc, l, lam, g, lam_init):
    n = acc.shape[0] // 2
    o = acc[:n] / l[:n] - lam * (acc[n:] / l[n:])
    return _rmsnorm(o, g) * (1.0 - lam_init)


def _prompt_attn_kernel(slope_ref, q_ref, k_ref, v_ref, km_ref, vm_ref,
                        lq1_ref, lk1_ref, lq2_ref, lk2_ref, gcol_ref, o_ref,
                        kb_ref, vt_ref, kmb_ref, vmt_ref, q2_ref, m_ref, acc_ref,
                        mask_ref, sa_ref, sb_ref, *, tq, n_meta, lam_init):
    h = pl.program_id(0)
    i = pl.program_id(1)
    slope = slope_ref[h] * LOG2E
    n_chunks = kb_ref.shape[0]

    @pl.when(i == 0)
    def _():
        key = lax.broadcasted_iota(jnp.int32, (tq, HEAD_V), 0)
        lane = lax.broadcasted_iota(jnp.int32, (tq, HEAD_V), 1)
        kfeat = jnp.where(lane < 3, lax.div(key, POS_RADIX),
                          jnp.where(lane < 6, lax.rem(key, POS_RADIX),
                                    jnp.where(lane < 9, 1, 0))).astype(BF16)
        sub = lax.broadcasted_iota(jnp.int32, (ACC_ROWS - HEAD_V, tq), 0)
        ones_row = jnp.where(sub == 0, 1.0, 0.0).astype(BF16)
        for c in range(n_chunks):
            kb_ref[c, :, :HEAD_V] = k_ref[c * tq:(c + 1) * tq, :].astype(BF16)
            kb_ref[c, :, HEAD_V:] = kfeat
            vt_ref[c, :HEAD_V, :] = v_ref[c * tq:(c + 1) * tq, :].T.astype(BF16)
            vt_ref[c, HEAD_V:, :] = ones_row
        pad = jnp.zeros((HEAD_V - n_meta, HEAD_V), F32)
        kmb_ref[:, :HEAD_V] = jnp.concatenate([km_ref[...], pad], axis=0).astype(BF16)
        kmb_ref[:, HEAD_V:] = jnp.zeros((HEAD_V, HEAD_V), BF16)
        vmt_ref[:HEAD_V, :] = jnp.concatenate([vm_ref[...], pad], axis=0).T.astype(BF16)
        vmt_ref[HEAD_V:, :] = ones_row[:, :HEAD_V]
        key = lax.broadcasted_iota(jnp.int32, (tq, 2 * tq), 0)
        qry = lax.broadcasted_iota(jnp.int32, (tq, 2 * tq), 1)
        qry = jnp.where(qry >= tq, qry - tq, qry)
        mask_ref[...] = jnp.where(key <= qry, 0.0, MASK_VALUE)
        qrow = lax.broadcasted_iota(jnp.int32, (2 * tq, HEAD_V), 0)
        lane = lax.broadcasted_iota(jnp.int32, (2 * tq, HEAD_V), 1)
        qrow = jnp.where(qrow >= tq, qrow - tq, qrow).astype(F32)
        whole = jnp.where(lane < 3, slope * POS_RADIX, jnp.where(lane < 6, slope, -slope * qrow))
        piece1 = whole.astype(BF16).astype(F32)
        rest = whole - piece1
        piece2 = rest.astype(BF16).astype(F32)
        piece3 = rest - piece2
        third = lax.rem(lane, 3)
        qfeat = jnp.where(third == 0, piece1, jnp.where(third == 1, piece2, piece3))
        q2_ref[:, HEAD_V:] = jnp.where(lane < 9, qfeat, 0.0).astype(BF16)

    q2_ref[:, :HEAD_V] = _stack_maps(q_ref[...], QK_SCALE * LOG2E)

    def scores(c, s_ref):
        s_ref[...] = lax.dot_general(kb_ref[c], q2_ref[...], _NT, preferred_element_type=F32)

    scores(0, sa_ref)

    key = lax.broadcasted_iota(jnp.int32, (HEAD_V, 2 * tq), 0)
    qry = lax.broadcasted_iota(jnp.int32, (HEAD_V, 2 * tq), 1)
    qry = jnp.where(qry >= tq, qry - tq, qry)
    qpos = n_meta + i * tq + qry
    s = lax.dot_general(kmb_ref[...], q2_ref[...], _NT, preferred_element_type=F32)
    s = jnp.where(key < n_meta, s + slope * (key - qpos).astype(F32), MASK_VALUE)
    m0 = jnp.max(s, axis=0, keepdims=True)
    m_ref[...] = m0
    acc_ref[...] = jnp.dot(vmt_ref[...], jnp.exp2(s - m0).astype(BF16), preferred_element_type=F32)

    def softmax_update(c, s_ref, diagonal):
        s = s_ref[...]
        if diagonal:
            s = s + mask_ref[...]
        sigma = slope * (tq * (c - i)).astype(F32)
        m_old = m_ref[...]
        m_new = jnp.maximum(m_old, jnp.max(s, axis=0, keepdims=True) + sigma)
        p = jnp.exp2(s - (m_new - sigma)).astype(BF16)
        alpha = jnp.exp2(m_old - m_new)
        acc_ref[...] = alpha * acc_ref[...] + jnp.dot(vt_ref[c], p, preferred_element_type=F32)
        m_ref[...] = m_new

    bufs = (sa_ref, sb_ref)

    def stretch(first, n_plain, then_diagonal):
        for u in range(n_plain):
            scores(first + u + 1, bufs[(u + 1) % 2])
            softmax_update(first + u, bufs[u % 2], False)
        if then_diagonal:
            softmax_update(first + n_plain, bufs[n_plain % 2], True)

    def body(j, carry):
        stretch(CHUNKS_PER_LOOP_STEP * j, CHUNKS_PER_LOOP_STEP, False)
        return carry

    n_full = i // CHUNKS_PER_LOOP_STEP
    lax.fori_loop(0, n_full, body, 0)
    left = i - CHUNKS_PER_LOOP_STEP * n_full
    for n_plain in range(CHUNKS_PER_LOOP_STEP):
        @pl.when(left == n_plain)
        def _(n_plain=n_plain):
            stretch(i - n_plain, n_plain, True)

    lam = _diff_lambda(lq1_ref, lk1_ref, lq2_ref, lk2_ref, lam_init)
    acc = acc_ref[:HEAD_V, :]
    l = acc_ref[HEAD_V:HEAD_V + 1, :]
    o = acc[:, :tq] / l[:, :tq] - lam * (acc[:, tq:] / l[:, tq:])
    ms = jnp.mean(o * o, axis=0, keepdims=True)
    o = o * lax.rsqrt(ms + EPS) * gcol_ref[...] * (1.0 - lam_init)
    o_ref[...] = o.T.astype(o_ref.dtype)


def _prompt_attention(z_b, z_s, meta_row0, slopes, lam_vecs, subln_g, lam_init, n_meta, tq=512):
    seq = z_b.shape[0]
    tq = _row_tile(seq, tq)
    assert tq % 128 == 0 and meta_row0 % n_meta == 0 and n_meta <= HEAD_V
    assert tq <= POS_RADIX * POS_RADIX, "key index digits must be exact in bf16"
    meta_blk = meta_row0 // n_meta
    vec = pl.BlockSpec((1, HEAD_QK), lambda h, i: (0, 0))
    kern = functools.partial(_prompt_attn_kernel, tq=tq, n_meta=n_meta, lam_init=lam_init)
    return pl.pallas_call(
        kern,
        grid=(N_HEADS, seq // tq),
        in_specs=[pl.BlockSpec(memory_space=pltpu.SMEM),
                  pl.BlockSpec((tq, HEAD_V), lambda h, i: (i, h)),
                  pl.BlockSpec((seq, HEAD_V), lambda h, i: (0, N_HEADS + h)),
                  pl.BlockSpec((seq, HEAD_V), lambda h, i: (0, 2 * N_HEADS + h)),
                  pl.BlockSpec((n_meta, HEAD_V), lambda h, i: (meta_blk, N_HEADS + h)),
                  pl.BlockSpec((n_meta, HEAD_V), lambda h, i: (meta_blk, 2 * N_HEADS + h)),
                  vec, vec, vec, vec,
                  pl.BlockSpec((HEAD_V, 1), lambda h, i: (0, 0))],
        out_specs=pl.BlockSpec((tq, HEAD_V), lambda h, i: (i, h)),
        out_shape=jax.ShapeDtypeStruct((seq, D_ATTN), BF16),
        scratch_shapes=[pltpu.VMEM((seq // tq, tq, 2 * HEAD_V), BF16),
                        pltpu.VMEM((seq // tq, ACC_ROWS, tq), BF16),
                        pltpu.VMEM((HEAD_V, 2 * HEAD_V), BF16),
                        pltpu.VMEM((ACC_ROWS, HEAD_V), BF16),
                        pltpu.VMEM((2 * tq, 2 * HEAD_V), BF16),
                        pltpu.VMEM((1, 2 * tq), F32),
                        pltpu.VMEM((ACC_ROWS, 2 * tq), F32),
                        pltpu.VMEM((tq, 2 * tq), F32),
                        pltpu.VMEM((tq, 2 * tq), F32),
                        pltpu.VMEM((tq, 2 * tq), F32)],
        compiler_params=_params(2),
        name="prompt_attn",
    )(slopes, z_b, z_b, z_b, z_s, z_s, *lam_vecs, subln_g.reshape(HEAD_V, 1))


def _meta_attn_kernel(slope_ref, q_ref, k_ref, v_ref, lq1_ref, lk1_ref, lq2_ref, lk2_ref, g_ref,
                      o_ref, *, n_meta, lam_init):
    slope = slope_ref[pl.program_id(0)]
    q2 = _stack_maps(q_ref[...])
    row = lax.broadcasted_iota(jnp.int32, (2 * n_meta, n_meta), 0)
    col = lax.broadcasted_iota(jnp.int32, (2 * n_meta, n_meta), 1)
    row = jnp.where(row >= n_meta, row - n_meta, row)
    s = lax.dot_general(q2, k_ref[...].astype(BF16), _NT, preferred_element_type=F32)
    s = jnp.where(col <= row, s + slope * (col - row).astype(F32), MASK_VALUE)
    p = jnp.exp(s - jnp.max(s, axis=-1, keepdims=True))
    l = jnp.sum(p, axis=-1, keepdims=True)
    acc = jnp.dot(p.astype(BF16), v_ref[...].astype(BF16), preferred_element_type=F32)
    lam = _diff_lambda(lq1_ref, lk1_ref, lq2_ref, lk2_ref, lam_init)
    o_ref[...] = _diff_combine(acc, l, lam, g_ref[...], lam_init)


def _meta_attention(z_s, meta_row0, slopes, lam_vecs, subln_g, lam_init, n_meta):
    meta_blk = meta_row0 // n_meta
    vec = pl.BlockSpec((1, HEAD_QK), lambda h: (0, 0))
    kern = functools.partial(_meta_attn_kernel, n_meta=n_meta, lam_init=lam_init)
    return pl.pallas_call(
        kern,
        grid=(N_HEADS,),
        in_specs=[pl.BlockSpec(memory_space=pltpu.SMEM),
                  pl.BlockSpec((n_meta, HEAD_V), lambda h: (meta_blk, h)),
                  pl.BlockSpec((n_meta, HEAD_V), lambda h: (meta_blk, N_HEADS + h)),
                  pl.BlockSpec((n_meta, HEAD_V), lambda h: (meta_blk, 2 * N_HEADS + h)),
                  vec, vec, vec, vec,
                  pl.BlockSpec((1, HEAD_V), lambda h: (0, 0))],
        out_specs=pl.BlockSpec((n_meta, HEAD_V), lambda h: (0, h)),
        out_shape=jax.ShapeDtypeStruct((n_meta, D_ATTN), F32),
        compiler_params=_params(1),
        name="meta_attn",
    )(slopes, z_s, z_s, z_s, *lam_vecs, subln_g)


def _sample_attn_parts(slope_ref, q_ref, kn_ref, vn_ref, k_pages, v_pages, lam_refs, g_ref, o_ref,
                       q2_ref, m_ref, l_ref, acc_ref, bias_ref, *, page_size, ld, past, lam_init):
    lq1_ref, lk1_ref, lq2_ref, lk2_ref = lam_refs
    tk = len(k_pages) * page_size
    n_pairs = N_HEADS // 2
    hr = 2 * ld
    pr = 2 * hr
    rows = N_HEADS * hr
    pair_heads = [(j, j + n_pairs) for j in range(n_pairs)]

    slope_col = jnp.concatenate([jnp.full((hr, 1), slope_ref[h], F32)
                                 for pair in pair_heads for h in pair], axis=0)

    def row_head_and_query(shape):
        row = lax.broadcasted_iota(jnp.int32, shape, 0)
        return lax.rem(lax.div(row, hr), 2), lax.rem(row, ld)

    def prologue(first_step, seq_start):
        @pl.when(first_step)
        def _():
            rh, qi = row_head_and_query((rows, 2 * tk))
            col = lax.broadcasted_iota(jnp.int32, (rows, 2 * tk), 1)
            rel = (lax.div(col, 2) - qi).astype(F32)
            bias_ref[...] = jnp.where(lax.rem(col, 2) == rh, slope_col * rel, MASK_VALUE)

        @pl.when(seq_start)
        def _():
            for j, pair in enumerate(pair_heads):
                q2_ref[j] = jnp.concatenate(
                    [_stack_maps(q_ref[:, h * HEAD_V:(h + 1) * HEAD_V]) for h in pair], axis=0)
            m_ref[...] = jnp.full(m_ref.shape, MASK_VALUE, F32)
            l_ref[...] = jnp.zeros(l_ref.shape, F32)
            acc_ref[...] = jnp.zeros(acc_ref.shape, F32)

    def update(s, values):
        m_old = m_ref[...]
        m_new = jnp.maximum(m_old, jnp.max(s, axis=-1, keepdims=True))
        p = jnp.exp(s - m_new)
        alpha = jnp.exp(m_old - m_new)
        l_ref[...] = alpha * l_ref[...] + jnp.sum(p, axis=-1, keepdims=True)
        p = p.astype(BF16)
        pv = [jnp.dot(p[j * pr:(j + 1) * pr], values[j], preferred_element_type=F32)
              for j in range(n_pairs)]
        acc_ref[...] = alpha * acc_ref[...] + jnp.concatenate(pv, axis=0)
        m_ref[...] = m_new

    def pair_rows(pages, j):
        return jnp.concatenate([pg[0, pl.ds(j, 2 * page_size, stride=n_pairs), :] for pg in pages],
                               axis=0).astype(BF16)

    def main(c, between=None):
        s_parts = [lax.dot_general(q2_ref[j], pair_rows(k_pages, j), _NT, preferred_element_type=F32)
                   for j in range(n_pairs)]
        if between is not None:
            between()
        values = [pair_rows(v_pages, j) for j in range(n_pairs)]
        group_offset = slope_col * (c * tk - past).astype(F32)
        update(jnp.concatenate(s_parts, axis=0) + bias_ref[...] + group_offset, values)

    def epilogue(seq_end):
        @pl.when(seq_end)
        def _():
            lam = _diff_lambda(lq1_ref, lk1_ref, lq2_ref, lk2_ref, lam_init)
            rh, qi = row_head_and_query((rows, hr))
            col = lax.broadcasted_iota(jnp.int32, (rows, hr), 1)
            kj = lax.rem(col, ld)
            visible = (lax.div(col, ld) == rh) & (kj <= qi)
            s_parts, values = [], []
            for j, pair in enumerate(pair_heads):
                kn = jnp.concatenate([kn_ref[:, h * HEAD_V:(h + 1) * HEAD_V] for h in pair], axis=0)
                vn = jnp.concatenate([vn_ref[:, h * HEAD_V:(h + 1) * HEAD_V] for h in pair], axis=0)
                s_parts.append(lax.dot_general(q2_ref[j], kn.astype(BF16), _NT,
                                               preferred_element_type=F32))
                values.append(vn.astype(BF16))
            s = jnp.concatenate(s_parts, axis=0) + slope_col * (kj - qi).astype(F32)
            update(jnp.where(visible, s, MASK_VALUE), values)
            acc = acc_ref[...]
            l = l_ref[...]
            for j, pair in enumerate(pair_heads):
                for t, h in enumerate(pair):
                    r0 = j * pr + t * hr
                    o_ref[:, h * HEAD_V:(h + 1) * HEAD_V] = _diff_combine(
                        acc[r0:r0 + hr], l[r0:r0 + hr], lam, g_ref[...], lam_init)

    return prologue, main, epilogue


def _window_sums(ext):
    out = {}
    s = ext
    w = 1
    while w < max(POOL_WINDOWS):
        s = s + pltpu.roll(s, w, 0)
        w *= 2
        out[w] = s
    return out


def _pool_project(u, sums, inv_cnt, wp_ref, scale_ref):
    cg = u.shape[1] // len(POOL_WINDOWS)
    outs = []
    for g, w in enumerate(POOL_WINDOWS):
        sl = slice(g * cg, (g + 1) * cg)
        d = sums[w][:, sl] * inv_cnt[w] - u[:, sl]
        outs.append(jnp.dot(d.astype(BF16), wp_ref[g], preferred_element_type=F32))
    return jnp.concatenate(outs, axis=1) * scale_ref[...]


def _pool_big_kernel(u_ref, prev_ref, meta_ref, wp_ref, scale_ref, m_ref):
    halo = jnp.where(pl.program_id(0) == 0, meta_ref[...], prev_ref[...])
    u = u_ref[...]
    sums = _window_sums(jnp.concatenate([halo, u], axis=0))
    sums = {w: s[POOL_HALO:] for w, s in sums.items()}
    inv = {w: 1.0 / w for w in POOL_WINDOWS}
    m_ref[...] = _pool_project(u, sums, inv, wp_ref, scale_ref).astype(m_ref.dtype)


def _pool_big(z_b, z_s, meta_row0, w_pool, pool_scale, tm=512):
    seq = z_b.shape[0]
    d_pool = w_pool.shape[0] * w_pool.shape[1]
    tm = _row_tile(seq, tm)
    ucol = z_b.shape[1] // d_pool - 1
    per = tm // POOL_HALO
    return pl.pallas_call(
        _pool_big_kernel,
        grid=(seq // tm,),
        in_specs=[pl.BlockSpec((tm, d_pool), lambda i: (i, ucol)),
                  pl.BlockSpec((POOL_HALO, d_pool), lambda i: (jnp.maximum(i * per - 1, 0), ucol)),
                  pl.BlockSpec((POOL_HALO, d_pool), lambda i: (meta_row0 // POOL_HALO, ucol)),
                  pl.BlockSpec(w_pool.shape, lambda i: (0, 0, 0)),
                  pl.BlockSpec((1, d_pool), lambda i: (0, 0))],
        out_specs=pl.BlockSpec((tm, d_pool), lambda i: (i, 0)),
        out_shape=jax.ShapeDtypeStruct((seq, d_pool), BF16),
        compiler_params=_params(1),
        name="pool_big",
    )(z_b, z_b, z_s, w_pool, pool_scale)


def _pool_small_kernel(ext_ref, wp_ref, scale_ref, m_ref, *, bd, ld, n_meta):
    grp = POOL_HALO + ld
    ext = ext_ref[...]
    sums = _window_sums(ext)
    meta0 = bd * grp + POOL_HALO

    def new_rows(a):
        parts = [a[b * grp + POOL_HALO:(b + 1) * grp] for b in range(bd)]
        return jnp.concatenate(parts + [a[meta0:meta0 + n_meta]], axis=0)

    n = bd * ld + n_meta
    r = lax.broadcasted_iota(jnp.int32, (n, 1), 0)
    inv = {}
    for w in POOL_WINDOWS:
        cnt = jnp.where(r < bd * ld, w, jnp.minimum(w, r - bd * ld + 1))
        inv[w] = 1.0 / cnt.astype(F32)
    sums = {w: new_rows(s) for w, s in sums.items()}
    m_ref[...] = _pool_project(new_rows(ext), sums, inv, wp_ref, scale_ref).astype(m_ref.dtype)


def _pool_small(ext, w_pool, pool_scale, bd, ld, n_meta):
    n = bd * ld + n_meta
    d_pool = ext.shape[1]
    kern = functools.partial(_pool_small_kernel, bd=bd, ld=ld, n_meta=n_meta)
    return pl.pallas_call(
        kern,
        grid=(1,),
        in_specs=[pl.BlockSpec(ext.shape, lambda i: (0, 0)),
                  pl.BlockSpec(w_pool.shape, lambda i: (0, 0, 0)),
                  pl.BlockSpec((1, d_pool), lambda i: (0, 0))],
        out_specs=pl.BlockSpec((n, d_pool), lambda i: (0, 0)),
        out_shape=jax.ShapeDtypeStruct((n, d_pool), BF16),
        compiler_params=_params(1),
        name="pool_small",
    )(ext, w_pool, pool_scale)


def _out_proj_kernel(x_ref, a_ref, m_ref, wa_ref, wm_ref, h_ref):
    h_ref[...] = (x_ref[...]
                  + jnp.dot(a_ref[...].astype(BF16), wa_ref[...], preferred_element_type=F32)
                  + jnp.dot(m_ref[...], wm_ref[...], preferred_element_type=F32))


def _out_proj(x, a, m, w_out, tm=512):
    rows, d = x.shape
    da, dm = a.shape[1], m.shape[1]
    tm = _row_tile(rows, tm)
    return pl.pallas_call(
        _out_proj_kernel,
        grid=(rows // tm,),
        in_specs=[pl.BlockSpec((tm, d), lambda i: (i, 0)),
                  pl.BlockSpec((tm, da), lambda i: (i, 0)),
                  pl.BlockSpec((tm, dm), lambda i: (i, 0)),
                  pl.BlockSpec((da, d), lambda i: (0, 0)),
                  pl.BlockSpec((dm, d), lambda i: (da // dm, 0))],
        out_specs=pl.BlockSpec((tm, d), lambda i: (i, 0)),
        out_shape=jax.ShapeDtypeStruct((rows, d), F32),
        compiler_params=_params(1),
        name="out_proj",
    )(x, a, m, w_out, w_out)


def _mlp_parts(h_ref, g_ref, wu_ref, wd_ref, gf_ref, y_ref, xn_ref, acc_ref):
    j = pl.program_id(1)

    def prologue():
        @pl.when(j == 0)
        def _():
            h = h_ref[...]
            xn_ref[...] = _rmsnorm(h, g_ref[...]).astype(BF16)
            acc_ref[...] = h

    def main():
        a = jnp.maximum(jnp.dot(xn_ref[...], wu_ref[...], preferred_element_type=F32), 0.0)
        acc_ref[...] += jnp.dot((a * a).astype(BF16), wd_ref[...], preferred_element_type=F32)

    def epilogue():
        @pl.when(j == pl.num_programs(1) - 1)
        def _():
            y_ref[...] = _rmsnorm(acc_ref[...], gf_ref[...])

    return prologue, main, epilogue


def _mlp_kernel(h_ref, g_ref, wu_ref, wd_ref, gf_ref, y_ref, xn_ref):
    prologue, main, epilogue = _mlp_parts(h_ref, g_ref, wu_ref, wd_ref, gf_ref, y_ref, xn_ref, y_ref)
    prologue()
    main()
    epilogue()


def _mlp_sample_kernel(pt_ref, h_ref, g_ref, wu_ref, wd_ref, gf_ref, slope_ref, q_ref, kn_ref, vn_ref,
                       *rest, pages_per_step, steps_per_seq, n_sample_steps, page_size, ld, past,
                       lam_init):
    del pt_ref
    P = pages_per_step
    k_pages, v_pages = rest[:P], rest[P:2 * P]
    lam_refs = rest[2 * P:2 * P + 4]
    (g_sub_ref, y_ref, o_ref, xn_ref, yacc_ref,
     q2_ref, m_ref, l_ref, acc_ref, bias_ref) = rest[2 * P + 4:]
    r = pl.program_id(2)
    t = (pl.program_id(0) * pl.num_programs(1) + pl.program_id(1)) * pl.num_programs(2) + r
    active = t < n_sample_steps
    c = lax.rem(jnp.minimum(t, n_sample_steps - 1), steps_per_seq)
    mlp_prologue, mlp_main, mlp_epilogue = _mlp_parts(h_ref, g_ref, wu_ref, wd_ref, gf_ref, y_ref,
                                                      xn_ref.at[r], yacc_ref.at[r])
    att_prologue, att_main, att_epilogue = _sample_attn_parts(
        slope_ref, q_ref, kn_ref, vn_ref, k_pages, v_pages, lam_refs, g_sub_ref, o_ref,
        q2_ref, m_ref, l_ref, acc_ref, bias_ref, page_size=page_size, ld=ld, past=past,
        lam_init=lam_init)
    mlp_prologue()
    att_prologue(t == 0, active & (c == 0))
    att_main(c, mlp_main)
    mlp_epilogue()
    att_epilogue(active & (c == steps_per_seq - 1))


def _mlp_with_sample_attention(h, g, w_up, w_down, g_final, z_s, cache_k, cache_v, page_table, slopes,
                               lam_vecs, subln_g, lam_init, ld, tm=512, tf=512):
    rows, d = h.shape
    d_ff = w_up.shape[1]
    tm = _row_tile(rows, tm)
    R = MLP_ROW_TILES_PER_WEIGHT_FETCH if (rows // tm) % MLP_ROW_TILES_PER_WEIGHT_FETCH == 0 else 1
    n_i, n_j = rows // (tm * R), d_ff // tf
    bd, n_pages = page_table.shape
    n_phys, page_size = cache_k.shape[0], cache_k.shape[1]
    past = n_pages * page_size
    P = min(p for p in range(1, n_pages + 1)
            if n_pages % p == 0 and bd * (n_pages // p) <= n_i * n_j * R)
    assert P <= MAX_PAGES_PER_STEP
    spp = n_pages // P
    n_steps = bd * spp
    k2 = cache_k.reshape(n_phys, page_size * N_HEADS, HEAD_V)
    v2 = cache_v.reshape(n_phys, page_size * N_HEADS, HEAD_V)

    def seq_and_group(i, j, r):
        t = jnp.minimum((i * n_j + j) * R + r, n_steps - 1)
        return t // spp, t % spp

    def seq_spec(col):
        return pl.BlockSpec((ld, D_ATTN), lambda i, j, r, pt: (seq_and_group(i, j, r)[0], col))

    def page_spec(k):
        def index(i, j, r, pt):
            b, c = seq_and_group(i, j, r)
            return pt[b * n_pages + c * P + k], 0, 0
        return pl.BlockSpec((1, page_size * N_HEADS, HEAD_V), index)

    def h_index(i, j, r, pt):
        return jnp.where(j == 0, i * R + r, i * R + R - 1), 0

    def y_index(i, j, r, pt):
        return jnp.where(j == n_j - 1, i * R + r, i * R), 0

    vec = pl.BlockSpec((1, HEAD_QK), lambda i, j, r, pt: (0, 0))
    kern = functools.partial(_mlp_sample_kernel, pages_per_step=P, steps_per_seq=spp,
                             n_sample_steps=n_steps, page_size=page_size, ld=ld, past=past,
                             lam_init=lam_init)
    score_rows = N_HEADS * 2 * ld
    grid_spec = pltpu.PrefetchScalarGridSpec(
        num_scalar_prefetch=1,
        grid=(n_i, n_j, R),
        in_specs=[pl.BlockSpec((tm, d), h_index, pipeline_mode=pl.Buffered(1)),
                  pl.BlockSpec((1, d), lambda i, j, r, pt: (0, 0)),
                  pl.BlockSpec((d, tf), lambda i, j, r, pt: (0, j)),
                  pl.BlockSpec((tf, d), lambda i, j, r, pt: (j, 0)),
                  pl.BlockSpec((1, d), lambda i, j, r, pt: (0, 0)),
                  pl.BlockSpec(memory_space=pltpu.SMEM),
                  seq_spec(0), seq_spec(1), seq_spec(2)]
                 + [page_spec(k) for k in range(P)] + [page_spec(k) for k in range(P)]
                 + [vec, vec, vec, vec, pl.BlockSpec((1, HEAD_V), lambda i, j, r, pt: (0, 0))],
        out_specs=[pl.BlockSpec((tm, d), y_index),
                   pl.BlockSpec((ld, D_ATTN), lambda i, j, r, pt: (seq_and_group(i, j, r)[0], 0))],
        scratch_shapes=[pltpu.VMEM((R, tm, d), BF16),
                        pltpu.VMEM((R, tm, d), F32),
                        pltpu.VMEM((N_HEADS // 2, 4 * ld, HEAD_V), BF16),
                        pltpu.VMEM((score_rows, 1), F32),
                        pltpu.VMEM((score_rows, 1), F32),
                        pltpu.VMEM((score_rows, HEAD_V), F32),
                        pltpu.VMEM((score_rows, 2 * P * page_size), F32)])
    return pl.pallas_call(
        kern,
        grid_spec=grid_spec,
        out_shape=[jax.ShapeDtypeStruct((rows, d), F32),
                   jax.ShapeDtypeStruct((bd * ld, D_ATTN), F32)],
        compiler_params=_params(3),
        name="mlp_sample_attn",
    )(page_table.reshape(-1), h, g, w_up, w_down, g_final, slopes, z_s, z_s, z_s,
      *([k2] * P), *([v2] * P), *lam_vecs, subln_g)


def _mlp(h, g, w_up, w_down, g_final, tm=512, tf=1024):
    rows, d = h.shape
    d_ff = w_up.shape[1]
    tm = _row_tile(rows, tm)
    return pl.pallas_call(
        _mlp_kernel,
        grid=(rows // tm, d_ff // tf),
        in_specs=[pl.BlockSpec((tm, d), lambda i, j: (i, 0)),
                  pl.BlockSpec((1, d), lambda i, j: (0, 0)),
                  pl.BlockSpec((d, tf), lambda i, j: (0, j)),
                  pl.BlockSpec((tf, d), lambda i, j: (j, 0)),
                  pl.BlockSpec((1, d), lambda i, j: (0, 0))],
        out_specs=pl.BlockSpec((tm, d), lambda i, j: (i, 0)),
        out_shape=jax.ShapeDtypeStruct((rows, d), F32),
        scratch_shapes=[pltpu.VMEM((tm, d), BF16)],
        compiler_params=_params(2),
        name="mlp",
    )(h, g, w_up, w_down, g_final)


def kernel(x_prompt, x_sample, cache_k, cache_v, state_pool, page_table, meta_tokens, norm_mix_g,
           w_in, lambda_q1, lambda_k1, lambda_q2, lambda_k2, subln_g, w_pool, pool_scale, w_out,
           norm_mlp_g, w_up, w_down, norm_final_g):
    depth = norm_mix_g.shape[0]
    batch, seq, d_model = x_prompt.shape
    bd, ld, _ = x_sample.shape
    n_meta = meta_tokens.shape[0]
    assert depth == 1 and batch == 1, "one layer and one prompt sequence are supported"
    assert n_meta == POOL_HALO and (bd * ld) % n_meta == 0 and seq >= POOL_HALO
    layer = 0
    lam_init = 0.8 - 0.6 * math.exp(-0.3 * layer)
    slopes = 2.0 ** (-8.0 * jnp.arange(1, N_HEADS + 1, dtype=F32) / N_HEADS)
    n_small = bd * ld
    d_pool = w_pool.shape[1] * w_pool.shape[2]

    x_big = x_prompt.reshape(seq, d_model)
    x_small = jnp.concatenate([x_sample.reshape(n_small, d_model), meta_tokens.astype(F32)], axis=0)

    g_mix = norm_mix_g[layer].reshape(1, d_model)
    g_mlp = norm_mlp_g[layer].reshape(1, d_model)
    g_final = norm_final_g.reshape(1, d_model)
    g_sub = subln_g[layer].reshape(1, HEAD_V)
    scale_pool = pool_scale[layer].reshape(1, d_pool)
    lam_vecs = [v[layer].reshape(1, HEAD_QK) for v in (lambda_q1, lambda_k1, lambda_q2, lambda_k2)]
    w_in_b = w_in[layer].astype(BF16)
    w_out_b = w_out[layer].astype(BF16)
    w_up_b = w_up[layer].astype(BF16)
    w_down_b = w_down[layer].astype(BF16)
    w_pool_b = w_pool[layer].astype(BF16)

    z_b, k_b, v_b = _in_proj(x_big, g_mix, w_in_b)
    z_s, k_s, v_s = _in_proj(x_small, g_mix, w_in_b)

    a_b = _prompt_attention(z_b, z_s, n_small, slopes, lam_vecs, g_sub, lam_init, n_meta)
    a_meta = _meta_attention(z_s, n_small, slopes, lam_vecs, g_sub, lam_init, n_meta)

    u_b = z_b[:, 3 * D_ATTN:]
    u_s = z_s[:n_small, 3 * D_ATTN:].reshape(bd, ld, d_pool)
    u_meta = z_s[n_small:, 3 * D_ATTN:]
    hist = state_pool[layer].astype(F32)
    n_hist = hist.shape[1]
    zpad = jnp.zeros((bd, POOL_HALO - n_hist, d_pool), F32)
    ext = jnp.concatenate([jnp.concatenate([zpad, hist, u_s], axis=1).reshape(-1, d_pool),
                           jnp.zeros((POOL_HALO, d_pool), F32), u_meta], axis=0)
    m_b = _pool_big(z_b, z_s, n_small, w_pool_b, scale_pool)
    m_s = _pool_small(ext, w_pool_b, scale_pool, bd, ld, n_meta)

    h_b = _out_proj(x_big, a_b, m_b, w_out_b)
    y_b, a_samp = _mlp_with_sample_attention(h_b, g_mlp, w_up_b, w_down_b, g_final, z_s, cache_k[layer],
                                             cache_v[layer], page_table, slopes, lam_vecs, g_sub,
                                             lam_init, ld)
    a_s = jnp.concatenate([a_samp, a_meta], axis=0)
    h_s = _out_proj(x_small, a_s, m_s, w_out_b)
    y_s = _mlp(h_s, g_mlp, w_up_b, w_down_b, g_final)

    t = seq + n_meta
    k_prompt = jnp.concatenate([k_s[n_small:], k_b], axis=0)
    v_prompt = jnp.concatenate([v_s[n_small:], v_b], axis=0)
    u_ext = jnp.concatenate([hist, u_s], axis=1)
    return (y_b.reshape(1, seq, d_model),
            y_s[:n_small].reshape(bd, ld, d_model),
            k_prompt.reshape(1, 1, t, N_HEADS, HEAD_V),
            v_prompt.reshape(1, 1, t, N_HEADS, HEAD_V),
            u_b[-n_hist:].reshape(1, 1, n_hist, d_pool),
            k_s[:n_small].reshape(1, bd, ld, N_HEADS, HEAD_V),
            v_s[:n_small].reshape(1, bd, ld, N_HEADS, HEAD_V),
            u_ext[:, -n_hist:].reshape(1, bd, n_hist, d_pool))
```

```python
import functools
import math

import jax
import jax.numpy as jnp
from jax import lax
from jax.experimental import pallas as pl
from jax.experimental.pallas import tpu as pltpu

N_HEADS = 8
HEAD_V = 128
HEAD_QK = HEAD_V // 2
D_ATTN = N_HEADS * HEAD_V
POOL_WINDOWS = (2, 4, 8, 16)
POOL_HALO = 16
EPS = 1e-6
MASK_VALUE = -1e30
QK_SCALE = HEAD_QK ** -0.5
LOG2E = 1.4426950408889634
POS_RADIX = 32
ACC_ROWS = HEAD_V + 16
CHUNKS_PER_LOOP_STEP = 8
assert CHUNKS_PER_LOOP_STEP % 2 == 0

VMEM_LIMIT_BYTES = 56 * 1024 * 1024
MAX_PAGES_PER_STEP = 16
MLP_ROW_TILES_PER_WEIGHT_FETCH = 2
BF16 = jnp.bfloat16
F32 = jnp.float32

_NT = (((1,), (1,)), ((), ()))


def _params(n_grid_dims):
    return pltpu.CompilerParams(dimension_semantics=("arbitrary",) * n_grid_dims,
                                vmem_limit_bytes=VMEM_LIMIT_BYTES)


def _row_tile(rows, target):
    best = rows
    for t in range(16, min(rows, target) + 1, 16):
        if rows % t == 0:
            best = t
    return best if best <= target else rows


def _rmsnorm(x, g):
    ms = jnp.mean(x * x, axis=-1, keepdims=True)
    return x * lax.rsqrt(ms + EPS) * g


def _in_proj_kernel(x_ref, g_ref, w_ref, z_ref, k_ref, v_ref, *, n_parts):
    tm = x_ref.shape[0]
    part = tm // n_parts
    for s in range(n_parts):
        rows = slice(s * part, (s + 1) * part)
        xn = _rmsnorm(x_ref[rows, :], g_ref[...]).astype(BF16)
        z = jnp.dot(xn, w_ref[...], preferred_element_type=F32)
        z_ref[rows, :] = z
        for out_ref, col in ((k_ref, D_ATTN), (v_ref, 2 * D_ATTN)):
            for h in range(N_HEADS):
                out_ref[pl.ds(s * part * N_HEADS + h, part, stride=N_HEADS), :] = (
                    z[:, col + h * HEAD_V:col + (h + 1) * HEAD_V])


def _in_proj(x, g, w, tm_target=512, n_parts=2):
    rows, d = x.shape
    n = w.shape[1]
    tm = _row_tile(rows, tm_target)
    if (tm // n_parts) % 16:
        n_parts = 1
    per_head = jax.ShapeDtypeStruct((rows * N_HEADS, HEAD_V), F32)
    head_spec = pl.BlockSpec((tm * N_HEADS, HEAD_V), lambda i: (i, 0))
    z, k, v = pl.pallas_call(
        functools.partial(_in_proj_kernel, n_parts=n_parts),
        grid=(rows // tm,),
        in_specs=[pl.BlockSpec((tm, d), lambda i: (i, 0)),
                  pl.BlockSpec((1, d), lambda i: (0, 0)),
                  pl.BlockSpec((d, n), lambda i: (0, 0), pipeline_mode=pl.Buffered(1))],
        out_specs=[pl.BlockSpec((tm, n), lambda i: (i, 0)), head_spec, head_spec],
        out_shape=[jax.ShapeDtypeStruct((rows, n), F32), per_head, per_head],
        compiler_params=_params(1),
        name="in_proj",
    )(x, g, w)
    return z, k.reshape(rows, N_HEADS, HEAD_V), v.reshape(rows, N_HEADS, HEAD_V)


def _stack_maps(q, scale=QK_SCALE):
    q = q * scale
    lane = lax.broadcasted_iota(jnp.int32, q.shape, 1)
    q1 = jnp.where(lane < HEAD_QK, q, 0.0)
    q2 = jnp.where(lane >= HEAD_QK, q, 0.0)
    return jnp.concatenate([q1, q2], axis=0).astype(BF16)


def _diff_lambda(lq1_ref, lk1_ref, lq2_ref, lk2_ref, lam_init):
    a = jnp.sum(lq1_ref[...] * lk1_ref[...], axis=-1, keepdims=True)
    b = jnp.sum(lq2_ref[...] * lk2_ref[...], axis=-1, keepdims=True)
    return jnp.exp(a) - jnp.exp(b) + lam_init


def _diff_combine(acc, l, lam, g, lam_init):
    n = acc.shape[0] // 2
    o = acc[:n] / l[:n] - lam * (acc[n:] / l[n:])
    return _rmsnorm(o, g) * (1.0 - lam_init)


def _prompt_attn_kernel(slope_ref, q_ref, k_ref, v_ref, km_ref, vm_ref,
                        lq1_ref, lk1_ref, lq2_ref, lk2_ref, gcol_ref, o_ref,
                        kb_ref, vt_ref, kmb_ref, vmt_ref, q2_ref, m_ref, acc_ref,
                        mask_ref, sa_ref, sb_ref, *, tq, n_meta, lam_init):
    h = pl.program_id(0)
    i = pl.program_id(1)
    slope = slope_ref[h] * LOG2E
    n_chunks = kb_ref.shape[0]

    @pl.when(i == 0)
    def _():
        key = lax.broadcasted_iota(jnp.int32, (tq, HEAD_V), 0)
        lane = lax.broadcasted_iota(jnp.int32, (tq, HEAD_V), 1)
        kfeat = jnp.where(lane < 3, lax.div(key, POS_RADIX),
                          jnp.where(lane < 6, lax.rem(key, POS_RADIX),
                                    jnp.where(lane < 9, 1, 0))).astype(BF16)
        sub = lax.broadcasted_iota(jnp.int32, (ACC_ROWS - HEAD_V, tq), 0)
        ones_row = jnp.where(sub == 0, 1.0, 0.0).astype(BF16)
        for c in range(n_chunks):
            kb_ref[c, :, :HEAD_V] = k_ref[c * tq:(c + 1) * tq, :].astype(BF16)
            kb_ref[c, :, HEAD_V:] = kfeat
            vt_ref[c, :HEAD_V, :] = v_ref[c * tq:(c + 1) * tq, :].T.astype(BF16)
            vt_ref[c, HEAD_V:, :] = ones_row
        pad = jnp.zeros((HEAD_V - n_meta, HEAD_V), F32)
        kmb_ref[:, :HEAD_V] = jnp.concatenate([km_ref[...], pad], axis=0).astype(BF16)
        kmb_ref[:, HEAD_V:] = jnp.zeros((HEAD_V, HEAD_V), BF16)
        vmt_ref[:HEAD_V, :] = jnp.concatenate([vm_ref[...], pad], axis=0).T.astype(BF16)
        vmt_ref[HEAD_V:, :] = ones_row[:, :HEAD_V]
        key = lax.broadcasted_iota(jnp.int32, (tq, 2 * tq), 0)
        qry = lax.broadcasted_iota(jnp.int32, (tq, 2 * tq), 1)
        qry = jnp.where(qry >= tq, qry - tq, qry)
        mask_ref[...] = jnp.where(key <= qry, 0.0, MASK_VALUE)
        qrow = lax.broadcasted_iota(jnp.int32, (2 * tq, HEAD_V), 0)
        lane = lax.broadcasted_iota(jnp.int32, (2 * tq, HEAD_V), 1)
        qrow = jnp.where(qrow >= tq, qrow - tq, qrow).astype(F32)
        whole = jnp.where(lane < 3, slope * POS_RADIX, jnp.where(lane < 6, slope, -slope * qrow))
        piece1 = whole.astype(BF16).astype(F32)
        rest = whole - piece1
        piece2 = rest.astype(BF16).astype(F32)
        piece3 = rest - piece2
        third = lax.rem(lane, 3)
        qfeat = jnp.where(third == 0, piece1, jnp.where(third == 1, piece2, piece3))
        q2_ref[:, HEAD_V:] = jnp.where(lane < 9, qfeat, 0.0).astype(BF16)

    q2_ref[:, :HEAD_V] = _stack_maps(q_ref[...], QK_SCALE * LOG2E)

    def scores(c, s_ref):
        s_ref[...] = lax.dot_general(kb_ref[c], q2_ref[...], _NT, preferred_element_type=F32)

    scores(0, sa_ref)

    key = lax.broadcasted_iota(jnp.int32, (HEAD_V, 2 * tq), 0)
    qry = lax.broadcasted_iota(jnp.int32, (HEAD_V, 2 * tq), 1)
    qry = jnp.where(qry >= tq, qry - tq, qry)
    qpos = n_meta + i * tq + qry
    s = lax.dot_general(kmb_ref[...], q2_ref[...], _NT, preferred_element_type=F32)
    s = jnp.where(key < n_meta, s + slope * (key - qpos).astype(F32), MASK_VALUE)
    m0 = jnp.max(s, axis=0, keepdims=True)
    m_ref[...] = m0
    acc_ref[...] = jnp.dot(vmt_ref[...], jnp.exp2(s - m0).astype(BF16), preferred_element_type=F32)

    def softmax_update(c, s_ref, diagonal):
        s = s_ref[...]
        if diagonal:
            s = s + mask_ref[...]
        sigma = slope * (tq * (c - i)).astype(F32)
        m_old = m_ref[...]
        m_new = jnp.maximum(m_old, jnp.max(s, axis=0, keepdims=True) + sigma)
        p = jnp.exp2(s - (m_new - sigma)).astype(BF16)
        alpha = jnp.exp2(m_old - m_new)
        acc_ref[...] = alpha * acc_ref[...] + jnp.dot(vt_ref[c], p, preferred_element_type=F32)
        m_ref[...] = m_new

    bufs = (sa_ref, sb_ref)

    def stretch(first, n_plain, then_diagonal):
        for u in range(n_plain):
            scores(first + u + 1, bufs[(u + 1) % 2])
            softmax_update(first + u, bufs[u % 2], False)
        if then_diagonal:
            softmax_update(first + n_plain, bufs[n_plain % 2], True)

    def body(j, carry):
        stretch(CHUNKS_PER_LOOP_STEP * j, CHUNKS_PER_LOOP_STEP, False)
        return carry

    n_full = i // CHUNKS_PER_LOOP_STEP
    lax.fori_loop(0, n_full, body, 0)
    left = i - CHUNKS_PER_LOOP_STEP * n_full
    for n_plain in range(CHUNKS_PER_LOOP_STEP):
        @pl.when(left == n_plain)
        def _(n_plain=n_plain):
            stretch(i - n_plain, n_plain, True)

    lam = _diff_lambda(lq1_ref, lk1_ref, lq2_ref, lk2_ref, lam_init)
    acc = acc_ref[:HEAD_V, :]
    l = acc_ref[HEAD_V:HEAD_V + 1, :]
    o = acc[:, :tq] / l[:, :tq] - lam * (acc[:, tq:] / l[:, tq:])
    ms = jnp.mean(o * o, axis=0, keepdims=True)
    o = o * lax.rsqrt(ms + EPS) * gcol_ref[...] * (1.0 - lam_init)
    o_ref[...] = o.T.astype(o_ref.dtype)


def _prompt_attention(z_b, z_s, meta_row0, slopes, lam_vecs, subln_g, lam_init, n_meta, tq=512):
    seq = z_b.shape[0]
    tq = _row_tile(seq, tq)
    assert tq % 128 == 0 and meta_row0 % n_meta == 0 and n_meta <= HEAD_V
    assert tq <= POS_RADIX * POS_RADIX, "key index digits must be exact in bf16"
    meta_blk = meta_row0 // n_meta
    vec = pl.BlockSpec((1, HEAD_QK), lambda h, i: (0, 0))
    kern = functools.partial(_prompt_attn_kernel, tq=tq, n_meta=n_meta, lam_init=lam_init)
    return pl.pallas_call(
        kern,
        grid=(N_HEADS, seq // tq),
        in_specs=[pl.BlockSpec(memory_space=pltpu.SMEM),
                  pl.BlockSpec((tq, HEAD_V), lambda h, i: (i, h)),
                  pl.BlockSpec((seq, HEAD_V), lambda h, i: (0, N_HEADS + h)),
                  pl.BlockSpec((seq, HEAD_V), lambda h, i: (0, 2 * N_HEADS + h)),
                  pl.BlockSpec((n_meta, HEAD_V), lambda h, i: (meta_blk, N_HEADS + h)),
                  pl.BlockSpec((n_meta, HEAD_V), lambda h, i: (meta_blk, 2 * N_HEADS + h)),
                  vec, vec, vec, vec,
                  pl.BlockSpec((HEAD_V, 1), lambda h, i: (0, 0))],
        out_specs=pl.BlockSpec((tq, HEAD_V), lambda h, i: (i, h)),
        out_shape=jax.ShapeDtypeStruct((seq, D_ATTN), BF16),
        scratch_shapes=[pltpu.VMEM((seq // tq, tq, 2 * HEAD_V), BF16),
                        pltpu.VMEM((seq // tq, ACC_ROWS, tq), BF16),
                        pltpu.VMEM((HEAD_V, 2 * HEAD_V), BF16),
                        pltpu.VMEM((ACC_ROWS, HEAD_V), BF16),
                        pltpu.VMEM((2 * tq, 2 * HEAD_V), BF16),
                        pltpu.VMEM((1, 2 * tq), F32),
                        pltpu.VMEM((ACC_ROWS, 2 * tq), F32),
                        pltpu.VMEM((tq, 2 * tq), F32),
                        pltpu.VMEM((tq, 2 * tq), F32),
                        pltpu.VMEM((tq, 2 * tq), F32)],
        compiler_params=_params(2),
        name="prompt_attn",
    )(slopes, z_b, z_b, z_b, z_s, z_s, *lam_vecs, subln_g.reshape(HEAD_V, 1))


def _meta_attn_kernel(slope_ref, q_ref, k_ref, v_ref, lq1_ref, lk1_ref, lq2_ref, lk2_ref, g_ref,
                      o_ref, *, n_meta, lam_init):
    slope = slope_ref[pl.program_id(0)]
    q2 = _stack_maps(q_ref[...])
    row = lax.broadcasted_iota(jnp.int32, (2 * n_meta, n_meta), 0)
    col = lax.broadcasted_iota(jnp.int32, (2 * n_meta, n_meta), 1)
    row = jnp.where(row >= n_meta, row - n_meta, row)
    s = lax.dot_general(q2, k_ref[...].astype(BF16), _NT, preferred_element_type=F32)
    s = jnp.where(col <= row, s + slope * (col - row).astype(F32), MASK_VALUE)
    p = jnp.exp(s - jnp.max(s, axis=-1, keepdims=True))
    l = jnp.sum(p, axis=-1, keepdims=True)
    acc = jnp.dot(p.astype(BF16), v_ref[...].astype(BF16), preferred_element_type=F32)
    lam = _diff_lambda(lq1_ref, lk1_ref, lq2_ref, lk2_ref, lam_init)
    o_ref[...] = _diff_combine(acc, l, lam, g_ref[...], lam_init)


def _meta_attention(z_s, meta_row0, slopes, lam_vecs, subln_g, lam_init, n_meta):
    meta_blk = meta_row0 // n_meta
    vec = pl.BlockSpec((1, HEAD_QK), lambda h: (0, 0))
    kern = functools.partial(_meta_attn_kernel, n_meta=n_meta, lam_init=lam_init)
    return pl.pallas_call(
        kern,
        grid=(N_HEADS,),
        in_specs=[pl.BlockSpec(memory_space=pltpu.SMEM),
                  pl.BlockSpec((n_meta, HEAD_V), lambda h: (meta_blk, h)),
                  pl.BlockSpec((n_meta, HEAD_V), lambda h: (meta_blk, N_HEADS + h)),
                  pl.BlockSpec((n_meta, HEAD_V), lambda h: (meta_blk, 2 * N_HEADS + h)),
                  vec, vec, vec, vec,
                  pl.BlockSpec((1, HEAD_V), lambda h: (0, 0))],
        out_specs=pl.BlockSpec((n_meta, HEAD_V), lambda h: (0, h)),
        out_shape=jax.ShapeDtypeStruct((n_meta, D_ATTN), F32),
        compiler_params=_params(1),
        name="meta_attn",
    )(slopes, z_s, z_s, z_s, *lam_vecs, subln_g)


def _sample_attn_parts(slope_ref, q_ref, kn_ref, vn_ref, k_pages, v_pages, lam_refs, g_ref, o_ref,
                       q2_ref, m_ref, l_ref, acc_ref, bias_ref, *, page_size, ld, past, lam_init):
    lq1_ref, lk1_ref, lq2_ref, lk2_ref = lam_refs
    tk = len(k_pages) * page_size
    n_pairs = N_HEADS // 2
    hr = 2 * ld
    pr = 2 * hr
    rows = N_HEADS * hr
    pair_heads = [(j, j + n_pairs) for j in range(n_pairs)]

    slope_col = jnp.concatenate([jnp.full((hr, 1), slope_ref[h], F32)
                                 for pair in pair_heads for h in pair], axis=0)

    def row_head_and_query(shape):
        row = lax.broadcasted_iota(jnp.int32, shape, 0)
        return lax.rem(lax.div(row, hr), 2), lax.rem(row, ld)

    def prologue(first_step, seq_start):
        @pl.when(first_step)
        def _():
            rh, qi = row_head_and_query((rows, 2 * tk))
            col = lax.broadcasted_iota(jnp.int32, (rows, 2 * tk), 1)
            rel = (lax.div(col, 2) - qi).astype(F32)
            bias_ref[...] = jnp.where(lax.rem(col, 2) == rh, slope_col * rel, MASK_VALUE)

        @pl.when(seq_start)
        def _():
            for j, pair in enumerate(pair_heads):
                q2_ref[j] = jnp.concatenate(
                    [_stack_maps(q_ref[:, h * HEAD_V:(h + 1) * HEAD_V]) for h in pair], axis=0)
            m_ref[...] = jnp.full(m_ref.shape, MASK_VALUE, F32)
            l_ref[...] = jnp.zeros(l_ref.shape, F32)
            acc_ref[...] = jnp.zeros(acc_ref.shape, F32)

    def update(s, values):
        m_old = m_ref[...]
        m_new = jnp.maximum(m_old, jnp.max(s, axis=-1, keepdims=True))
        p = jnp.exp(s - m_new)
        alpha = jnp.exp(m_old - m_new)
        l_ref[...] = alpha * l_ref[...] + jnp.sum(p, axis=-1, keepdims=True)
        p = p.astype(BF16)
        pv = [jnp.dot(p[j * pr:(j + 1) * pr], values[j], preferred_element_type=F32)
              for j in range(n_pairs)]
        acc_ref[...] = alpha * acc_ref[...] + jnp.concatenate(pv, axis=0)
        m_ref[...] = m_new

    def pair_rows(pages, j):
        return jnp.concatenate([pg[0, pl.ds(j, 2 * page_size, stride=n_pairs), :] for pg in pages],
                               axis=0).astype(BF16)

    def main(c, between=None):
        s_parts = [lax.dot_general(q2_ref[j], pair_rows(k_pages, j), _NT, preferred_element_type=F32)
                   for j in range(n_pairs)]
        if between is not None:
            between()
        values = [pair_rows(v_pages, j) for j in range(n_pairs)]
        group_offset = slope_col * (c * tk - past).astype(F32)
        update(jnp.concatenate(s_parts, axis=0) + bias_ref[...] + group_offset, values)

    def epilogue(seq_end):
        @pl.when(seq_end)
        def _():
            lam = _diff_lambda(lq1_ref, lk1_ref, lq2_ref, lk2_ref, lam_init)
            rh, qi = row_head_and_query((rows, hr))
            col = lax.broadcasted_iota(jnp.int32, (rows, hr), 1)
            kj = lax.rem(col, ld)
            visible = (lax.div(col, ld) == rh) & (kj <= qi)
            s_parts, values = [], []
            for j, pair in enumerate(pair_heads):
                kn = jnp.concatenate([kn_ref[:, h * HEAD_V:(h + 1) * HEAD_V] for h in pair], axis=0)
                vn = jnp.concatenate([vn_ref[:, h * HEAD_V:(h + 1) * HEAD_V] for h in pair], axis=0)
                s_parts.append(lax.dot_general(q2_ref[j], kn.astype(BF16), _NT,
                                               preferred_element_type=F32))
                values.append(vn.astype(BF16))
            s = jnp.concatenate(s_parts, axis=0) + slope_col * (kj - qi).astype(F32)
            update(jnp.where(visible, s, MASK_VALUE), values)
            acc = acc_ref[...]
            l = l_ref[...]
            for j, pair in enumerate(pair_heads):
                for t, h in enumerate(pair):
                    r0 = j * pr + t * hr
                    o_ref[:, h * HEAD_V:(h + 1) * HEAD_V] = _diff_combine(
                        acc[r0:r0 + hr], l[r0:r0 + hr], lam, g_ref[...], lam_init)

    return prologue, main, epilogue


def _window_sums(ext):
    out = {}
    s = ext
    w = 1
    while w < max(POOL_WINDOWS):
        s = s + pltpu.roll(s, w, 0)
        w *= 2
        out[w] = s
    return out


def _pool_project(u, sums, inv_cnt, wp_ref, scale_ref):
    cg = u.shape[1] // len(POOL_WINDOWS)
    outs = []
    for g, w in enumerate(POOL_WINDOWS):
        sl = slice(g * cg, (g + 1) * cg)
        d = sums[w][:, sl] * inv_cnt[w] - u[:, sl]
        outs.append(jnp.dot(d.astype(BF16), wp_ref[g], preferred_element_type=F32))
    return jnp.concatenate(outs, axis=1) * scale_ref[...]


def _pool_big_kernel(u_ref, prev_ref, meta_ref, wp_ref, scale_ref, m_ref):
    halo = jnp.where(pl.program_id(0) == 0, meta_ref[...], prev_ref[...])
    u = u_ref[...]
    sums = _window_sums(jnp.concatenate([halo, u], axis=0))
    sums = {w: s[POOL_HALO:] for w, s in sums.items()}
    inv = {w: 1.0 / w for w in POOL_WINDOWS}
    m_ref[...] = _pool_project(u, sums, inv, wp_ref, scale_ref).astype(m_ref.dtype)


def _pool_big(z_b, z_s, meta_row0, w_pool, pool_scale, tm=512):
    seq = z_b.shape[0]
    d_pool = w_pool.shape[0] * w_pool.shape[1]
    tm = _row_tile(seq, tm)
    ucol = z_b.shape[1] // d_pool - 1
    per = tm // POOL_HALO
    return pl.pallas_call(
        _pool_big_kernel,
        grid=(seq // tm,),
        in_specs=[pl.BlockSpec((tm, d_pool), lambda i: (i, ucol)),
                  pl.BlockSpec((POOL_HALO, d_pool), lambda i: (jnp.maximum(i * per - 1, 0), ucol)),
                  pl.BlockSpec((POOL_HALO, d_pool), lambda i: (meta_row0 // POOL_HALO, ucol)),
                  pl.BlockSpec(w_pool.shape, lambda i: (0, 0, 0)),
                  pl.BlockSpec((1, d_pool), lambda i: (0, 0))],
        out_specs=pl.BlockSpec((tm, d_pool), lambda i: (i, 0)),
        out_shape=jax.ShapeDtypeStruct((seq, d_pool), BF16),
        compiler_params=_params(1),
        name="pool_big",
    )(z_b, z_b, z_s, w_pool, pool_scale)


def _pool_small_kernel(ext_ref, wp_ref, scale_ref, m_ref, *, bd, ld, n_meta):
    grp = POOL_HALO + ld
    ext = ext_ref[...]
    sums = _window_sums(ext)
    meta0 = bd * grp + POOL_HALO

    def new_rows(a):
        parts = [a[b * grp + POOL_HALO:(b + 1) * grp] for b in range(bd)]
        return jnp.concatenate(parts + [a[meta0:meta0 + n_meta]], axis=0)

    n = bd * ld + n_meta
    r = lax.broadcasted_iota(jnp.int32, (n, 1), 0)
    inv = {}
    for w in POOL_WINDOWS:
        cnt = jnp.where(r < bd * ld, w, jnp.minimum(w, r - bd * ld + 1))
        inv[w] = 1.0 / cnt.astype(F32)
    sums = {w: new_rows(s) for w, s in sums.items()}
    m_ref[...] = _pool_project(new_rows(ext), sums, inv, wp_ref, scale_ref).astype(m_ref.dtype)


def _pool_small(ext, w_pool, pool_scale, bd, ld, n_meta):
    n = bd * ld + n_meta
    d_pool = ext.shape[1]
    kern = functools.partial(_pool_small_kernel, bd=bd, ld=ld, n_meta=n_meta)
    return pl.pallas_call(
        kern,
        grid=(1,),
        in_specs=[pl.BlockSpec(ext.shape, lambda i: (0, 0)),
                  pl.BlockSpec(w_pool.shape, lambda i: (0, 0, 0)),
                  pl.BlockSpec((1, d_pool), lambda i: (0, 0))],
        out_specs=pl.BlockSpec((n, d_pool), lambda i: (0, 0)),
        out_shape=jax.ShapeDtypeStruct((n, d_pool), BF16),
        compiler_params=_params(1),
        name="pool_small",
    )(ext, w_pool, pool_scale)


def _out_proj_kernel(x_ref, a_ref, m_ref, wa_ref, wm_ref, h_ref):
    h_ref[...] = (x_ref[...]
                  + jnp.dot(a_ref[...].astype(BF16), wa_ref[...], preferred_element_type=F32)
                  + jnp.dot(m_ref[...], wm_ref[...], preferred_element_type=F32))


def _out_proj(x, a, m, w_out, tm=512):
    rows, d = x.shape
    da, dm = a.shape[1], m.shape[1]
    tm = _row_tile(rows, tm)
    return pl.pallas_call(
        _out_proj_kernel,
        grid=(rows // tm,),
        in_specs=[pl.BlockSpec((tm, d), lambda i: (i, 0)),
                  pl.BlockSpec((tm, da), lambda i: (i, 0)),
                  pl.BlockSpec((tm, dm), lambda i: (i, 0)),
                  pl.BlockSpec((da, d), lambda i: (0, 0)),
                  pl.BlockSpec((dm, d), lambda i: (da // dm, 0))],
        out_specs=pl.BlockSpec((tm, d), lambda i: (i, 0)),
        out_shape=jax.ShapeDtypeStruct((rows, d), F32),
        compiler_params=_params(1),
        name="out_proj",
    )(x, a, m, w_out, w_out)


def _mlp_parts(h_ref, g_ref, wu_ref, wd_ref, gf_ref, y_ref, xn_ref, acc_ref):
    j = pl.program_id(1)

    def prologue():
        @pl.when(j == 0)
        def _():
            h = h_ref[...]
            xn_ref[...] = _rmsnorm(h, g_ref[...]).astype(BF16)
            acc_ref[...] = h

    def main():
        a = jnp.maximum(jnp.dot(xn_ref[...], wu_ref[...], preferred_element_type=F32), 0.0)
        acc_ref[...] += jnp.dot((a * a).astype(BF16), wd_ref[...], preferred_element_type=F32)

    def epilogue():
        @pl.when(j == pl.num_programs(1) - 1)
        def _():
            y_ref[...] = _rmsnorm(acc_ref[...], gf_ref[...])

    return prologue, main, epilogue


def _mlp_kernel(h_ref, g_ref, wu_ref, wd_ref, gf_ref, y_ref, xn_ref):
    prologue, main, epilogue = _mlp_parts(h_ref, g_ref, wu_ref, wd_ref, gf_ref, y_ref, xn_ref, y_ref)
    prologue()
    main()
    epilogue()


def _mlp_sample_kernel(pt_ref, h_ref, g_ref, wu_ref, wd_ref, gf_ref, slope_ref, q_ref, kn_ref, vn_ref,
                       *rest, pages_per_step, steps_per_seq, n_sample_steps, page_size, ld, past,
                       lam_init):
    del pt_ref
    P = pages_per_step
    k_pages, v_pages = rest[:P], rest[P:2 * P]
    lam_refs = rest[2 * P:2 * P + 4]
    (g_sub_ref, y_ref, o_ref, xn_ref, yacc_ref,
     q2_ref, m_ref, l_ref, acc_ref, bias_ref) = rest[2 * P + 4:]
    r = pl.program_id(2)
    t = (pl.program_id(0) * pl.num_programs(1) + pl.program_id(1)) * pl.num_programs(2) + r
    active = t < n_sample_steps
    c = lax.rem(jnp.minimum(t, n_sample_steps - 1), steps_per_seq)
    mlp_prologue, mlp_main, mlp_epilogue = _mlp_parts(h_ref, g_ref, wu_ref, wd_ref, gf_ref, y_ref,
                                                      xn_ref.at[r], yacc_ref.at[r])
    att_prologue, att_main, att_epilogue = _sample_attn_parts(
        slope_ref, q_ref, kn_ref, vn_ref, k_pages, v_pages, lam_refs, g_sub_ref, o_ref,
        q2_ref, m_ref, l_ref, acc_ref, bias_ref, page_size=page_size, ld=ld, past=past,
        lam_init=lam_init)
    mlp_prologue()
    att_prologue(t == 0, active & (c == 0))
    att_main(c, mlp_main)
    mlp_epilogue()
    att_epilogue(active & (c == steps_per_seq - 1))


def _mlp_with_sample_attention(h, g, w_up, w_down, g_final, z_s, cache_k, cache_v, page_table, slopes,
                               lam_vecs, subln_g, lam_init, ld, tm=512, tf=512):
    rows, d = h.shape
    d_ff = w_up.shape[1]
    tm = _row_tile(rows, tm)
    R = MLP_ROW_TILES_PER_WEIGHT_FETCH if (rows // tm) % MLP_ROW_TILES_PER_WEIGHT_FETCH == 0 else 1
    n_i, n_j = rows // (tm * R), d_ff // tf
    bd, n_pages = page_table.shape
    n_phys, page_size = cache_k.shape[0], cache_k.shape[1]
    past = n_pages * page_size
    P = min(p for p in range(1, n_pages + 1)
            if n_pages % p == 0 and bd * (n_pages // p) <= n_i * n_j * R)
    assert P <= MAX_PAGES_PER_STEP
    spp = n_pages // P
    n_steps = bd * spp
    k2 = cache_k.reshape(n_phys, page_size * N_HEADS, HEAD_V)
    v2 = cache_v.reshape(n_phys, page_size * N_HEADS, HEAD_V)

    def seq_and_group(i, j, r):
        t = jnp.minimum((i * n_j + j) * R + r, n_steps - 1)
        return t // spp, t % spp

    def seq_spec(col):
        return pl.BlockSpec((ld, D_ATTN), lambda i, j, r, pt: (seq_and_group(i, j, r)[0], col))

    def page_spec(k):
        def index(i, j, r, pt):
            b, c = seq_and_group(i, j, r)
            return pt[b * n_pages + c * P + k], 0, 0
        return pl.BlockSpec((1, page_size * N_HEADS, HEAD_V), index)

    def h_index(i, j, r, pt):
        return jnp.where(j == 0, i * R + r, i * R + R - 1), 0

    def y_index(i, j, r, pt):
        return jnp.where(j == n_j - 1, i * R + r, i * R), 0

    vec = pl.BlockSpec((1, HEAD_QK), lambda i, j, r, pt: (0, 0))
    kern = functools.partial(_mlp_sample_kernel, pages_per_step=P, steps_per_seq=spp,
                             n_sample_steps=n_steps, page_size=page_size, ld=ld, past=past,
                             lam_init=lam_init)
    score_rows = N_HEADS * 2 * ld
    grid_spec = pltpu.PrefetchScalarGridSpec(
        num_scalar_prefetch=1,
        grid=(n_i, n_j, R),
        in_specs=[pl.BlockSpec((tm, d), h_index, pipeline_mode=pl.Buffered(1)),
                  pl.BlockSpec((1, d), lambda i, j, r, pt: (0, 0)),
                  pl.BlockSpec((d, tf), lambda i, j, r, pt: (0, j)),
                  pl.BlockSpec((tf, d), lambda i, j, r, pt: (j, 0)),
                  pl.BlockSpec((1, d), lambda i, j, r, pt: (0, 0)),
                  pl.BlockSpec(memory_space=pltpu.SMEM),
                  seq_spec(0), seq_spec(1), seq_spec(2)]
                 + [page_spec(k) for k in range(P)] + [page_spec(k) for k in range(P)]
                 + [vec, vec, vec, vec, pl.BlockSpec((1, HEAD_V), lambda i, j, r, pt: (0, 0))],
        out_specs=[pl.BlockSpec((tm, d), y_index),
                   pl.BlockSpec((ld, D_ATTN), lambda i, j, r, pt: (seq_and_group(i, j, r)[0], 0))],
        scratch_shapes=[pltpu.VMEM((R, tm, d), BF16),
                        pltpu.VMEM((R, tm, d), F32),
                        pltpu.VMEM((N_HEADS // 2, 4 * ld, HEAD_V), BF16),
                        pltpu.VMEM((score_rows, 1), F32),
                        pltpu.VMEM((score_rows, 1), F32),
                        pltpu.VMEM((score_rows, HEAD_V), F32),
                        pltpu.VMEM((score_rows, 2 * P * page_size), F32)])
    return pl.pallas_call(
        kern,
        grid_spec=grid_spec,
        out_shape=[jax.ShapeDtypeStruct((rows, d), F32),
                   jax.ShapeDtypeStruct((bd * ld, D_ATTN), F32)],
        compiler_params=_params(3),
        name="mlp_sample_attn",
    )(page_table.reshape(-1), h, g, w_up, w_down, g_final, slopes, z_s, z_s, z_s,
      *([k2] * P), *([v2] * P), *lam_vecs, subln_g)


def _mlp(h, g, w_up, w_down, g_final, tm=512, tf=1024):
    rows, d = h.shape
    d_ff = w_up.shape[1]
    tm = _row_tile(rows, tm)
    return pl.pallas_call(
        _mlp_kernel,
        grid=(rows // tm, d_ff // tf),
        in_specs=[pl.BlockSpec((tm, d), lambda i, j: (i, 0)),
                  pl.BlockSpec((1, d), lambda i, j: (0, 0)),
                  pl.BlockSpec((d, tf), lambda i, j: (0, j)),
                  pl.BlockSpec((tf, d), lambda i, j: (j, 0)),
                  pl.BlockSpec((1, d), lambda i, j: (0, 0))],
        out_specs=pl.BlockSpec((tm, d), lambda i, j: (i, 0)),
        out_shape=jax.ShapeDtypeStruct((rows, d), F32),
        scratch_shapes=[pltpu.VMEM((tm, d), BF16)],
        compiler_params=_params(2),
        name="mlp",
    )(h, g, w_up, w_down, g_final)


def kernel(x_prompt, x_sample, cache_k, cache_v, state_pool, page_table, meta_tokens, norm_mix_g,
           w_in, lambda_q1, lambda_k1, lambda_q2, lambda_k2, subln_g, w_pool, pool_scale, w_out,
           norm_mlp_g, w_up, w_down, norm_final_g):
    depth = norm_mix_g.shape[0]
    batch, seq, d_model = x_prompt.shape
    bd, ld, _ = x_sample.shape
    n_meta = meta_tokens.shape[0]
    assert depth == 1 and batch == 1, "one layer and one prompt sequence are supported"
    assert n_meta == POOL_HALO and (bd * ld) % n_meta == 0 and seq >= POOL_HALO
    layer = 0
    lam_init = 0.8 - 0.6 * math.exp(-0.3 * layer)
    slopes = 2.0 ** (-8.0 * jnp.arange(1, N_HEADS + 1, dtype=F32) / N_HEADS)
    n_small = bd * ld
    d_pool = w_pool.shape[1] * w_pool.shape[2]

    x_big = x_prompt.reshape(seq, d_model)
    x_small = jnp.concatenate([x_sample.reshape(n_small, d_model), meta_tokens.astype(F32)], axis=0)

    g_mix = norm_mix_g[layer].reshape(1, d_model)
    g_mlp = norm_mlp_g[layer].reshape(1, d_model)
    g_final = norm_final_g.reshape(1, d_model)
    g_sub = subln_g[layer].reshape(1, HEAD_V)
    scale_pool = pool_scale[layer].reshape(1, d_pool)
    lam_vecs = [v[layer].reshape(1, HEAD_QK) for v in (lambda_q1, lambda_k1, lambda_q2, lambda_k2)]
    w_in_b = w_in[layer].astype(BF16)
    w_out_b = w_out[layer].astype(BF16)
    w_up_b = w_up[layer].astype(BF16)
    w_down_b = w_down[layer].astype(BF16)
    w_pool_b = w_pool[layer].astype(BF16)

    z_b, k_b, v_b = _in_proj(x_big, g_mix, w_in_b)
    z_s, k_s, v_s = _in_proj(x_small, g_mix, w_in_b)

    a_b = _prompt_attention(z_b, z_s, n_small, slopes, lam_vecs, g_sub, lam_init, n_meta)
    a_meta = _meta_attention(z_s, n_small, slopes, lam_vecs, g_sub, lam_init, n_meta)

    u_b = z_b[:, 3 * D_ATTN:]
    u_s = z_s[:n_small, 3 * D_ATTN:].reshape(bd, ld, d_pool)
    u_meta = z_s[n_small:, 3 * D_ATTN:]
    hist = state_pool[layer].astype(F32)
    n_hist = hist.shape[1]
    zpad = jnp.zeros((bd, POOL_HALO - n_hist, d_pool), F32)
    ext = jnp.concatenate([jnp.concatenate([zpad, hist, u_s], axis=1).reshape(-1, d_pool),
                           jnp.zeros((POOL_HALO, d_pool), F32), u_meta], axis=0)
    m_b = _pool_big(z_b, z_s, n_small, w_pool_b, scale_pool)
    m_s = _pool_small(ext, w_pool_b, scale_pool, bd, ld, n_meta)

    h_b = _out_proj(x_big, a_b, m_b, w_out_b)
    y_b, a_samp = _mlp_with_sample_attention(h_b, g_mlp, w_up_b, w_down_b, g_final, z_s, cache_k[layer],
                                             cache_v[layer], page_table, slopes, lam_vecs, g_sub,
                                             lam_init, ld)
    a_s = jnp.concatenate([a_samp, a_meta], axis=0)
    h_s = _out_proj(x_small, a_s, m_s, w_out_b)
    y_s = _mlp(h_s, g_mlp, w_up_b, w_down_b, g_final)

    t = seq + n_meta
    k_prompt = jnp.concatenate([k_s[n_small:], k_b], axis=0)
    v_prompt = jnp.concatenate([v_s[n_small:], v_b], axis=0)
    u_ext = jnp.concatenate([hist, u_s], axis=1)
    return (y_b.reshape(1, seq, d_model),
            y_s[:n_small].reshape(bd, ld, d_model),
            k_prompt.reshape(1, 1, t, N_HEADS, HEAD_V),
            v_prompt.reshape(1, 1, t, N_HEADS, HEAD_V),
            u_b[-n_hist:].reshape(1, 1, n_hist, d_pool),
            k_s[:n_small].reshape(1, bd, ld, N_HEADS, HEAD_V),
            v_s[:n_small].reshape(1, bd, ld, N_HEADS, HEAD_V),
            u_ext[:, -n_hist:].reshape(1, bd, n_hist, d_pool))
```

```python
import functools
import math

import jax
import jax.numpy as jnp
from jax import lax
from jax.experimental import pallas as pl
from jax.experimental.pallas import tpu as pltpu

N_HEADS = 8
HEAD_V = 128
HEAD_QK = HEAD_V // 2
D_ATTN = N_HEADS * HEAD_V
POOL_WINDOWS = (2, 4, 8, 16)
POOL_HALO = 16
EPS = 1e-6
MASK_VALUE = -1e30
QK_SCALE = HEAD_QK ** -0.5
LOG2E = 1.4426950408889634
POS_RADIX = 32
ACC_ROWS = HEAD_V + 16
CHUNKS_PER_LOOP_STEP = 8
assert CHUNKS_PER_LOOP_STEP % 2 == 0

VMEM_LIMIT_BYTES = 56 * 1024 * 1024
MAX_PAGES_PER_STEP = 16
MLP_ROW_TILES_PER_WEIGHT_FETCH = 2
BF16 = jnp.bfloat16
F32 = jnp.float32

_NT = (((1,), (1,)), ((), ()))


def _params(n_grid_dims):
    return pltpu.CompilerParams(dimension_semantics=("arbitrary",) * n_grid_dims,
                                vmem_limit_bytes=VMEM_LIMIT_BYTES)


def _row_tile(rows, target):
    best = rows
    for t in range(16, min(rows, target) + 1, 16):
        if rows % t == 0:
            best = t
    return best if best <= target else rows


def _rmsnorm(x, g):
    ms = jnp.mean(x * x, axis=-1, keepdims=True)
    return x * lax.rsqrt(ms + EPS) * g


def _in_proj_kernel(x_ref, g_ref, w_ref, z_ref, k_ref, v_ref, *, n_parts):
    tm = x_ref.shape[0]
    part = tm // n_parts
    for s in range(n_parts):
        rows = slice(s * part, (s + 1) * part)
        xn = _rmsnorm(x_ref[rows, :], g_ref[...]).astype(BF16)
        z = jnp.dot(xn, w_ref[...], preferred_element_type=F32)
        z_ref[rows, :] = z
        for out_ref, col in ((k_ref, D_ATTN), (v_ref, 2 * D_ATTN)):
            for h in range(N_HEADS):
                out_ref[pl.ds(s * part * N_HEADS + h, part, stride=N_HEADS), :] = (
                    z[:, col + h * HEAD_V:col + (h + 1) * HEAD_V])


def _in_proj(x, g, w, tm_target=512, n_parts=2):
    rows, d = x.shape
    n = w.shape[1]
    tm = _row_tile(rows, tm_target)
    if (tm // n_parts) % 16:
        n_parts = 1
    per_head = jax.ShapeDtypeStruct((rows * N_HEADS, HEAD_V), F32)
    head_spec = pl.BlockSpec((tm * N_HEADS, HEAD_V), lambda i: (i, 0))
    z, k, v = pl.pallas_call(
        functools.partial(_in_proj_kernel, n_parts=n_parts),
        grid=(rows // tm,),
        in_specs=[pl.BlockSpec((tm, d), lambda i: (i, 0)),
                  pl.BlockSpec((1, d), lambda i: (0, 0)),
                  pl.BlockSpec((d, n), lambda i: (0, 0), pipeline_mode=pl.Buffered(1))],
        out_specs=[pl.BlockSpec((tm, n), lambda i: (i, 0)), head_spec, head_spec],
        out_shape=[jax.ShapeDtypeStruct((rows, n), F32), per_head, per_head],
        compiler_params=_params(1),
        name="in_proj",
    )(x, g, w)
    return z, k.reshape(rows, N_HEADS, HEAD_V), v.reshape(rows, N_HEADS, HEAD_V)


def _stack_maps(q, scale=QK_SCALE):
    q = q * scale
    lane = lax.broadcasted_iota(jnp.int32, q.shape, 1)
    q1 = jnp.where(lane < HEAD_QK, q, 0.0)
    q2 = jnp.where(lane >= HEAD_QK, q, 0.0)
    return jnp.concatenate([q1, q2], axis=0).astype(BF16)


def _diff_lambda(lq1_ref, lk1_ref, lq2_ref, lk2_ref, lam_init):
    a = jnp.sum(lq1_ref[...] * lk1_ref[...], axis=-1, keepdims=True)
    b = jnp.sum(lq2_ref[...] * lk2_ref[...], axis=-1, keepdims=True)
    return jnp.exp(a) - jnp.exp(b) + lam_init


def _diff_combine(acc, l, lam, g, lam_init):
    n = acc.shape[0] // 2
    o = acc[:n] / l[:n] - lam * (acc[n:] / l[n:])
    return _rmsnorm(o, g) * (1.0 - lam_init)


def _prompt_attn_kernel(slope_ref, q_ref, k_ref, v_ref, km_ref, vm_ref,
                        lq1_ref, lk1_ref, lq2_ref, lk2_ref, gcol_ref, o_ref,
                        kb_ref, vt_ref, kmb_ref, vmt_ref, q2_ref, m_ref, acc_ref,
                        mask_ref, sa_ref, sb_ref, *, tq, n_meta, lam_init):
    h = pl.program_id(0)
    i = pl.program_id(1)
    slope = slope_ref[h] * LOG2E
    n_chunks = kb_ref.shape[0]

    @pl.when(i == 0)
    def _():
        key = lax.broadcasted_iota(jnp.int32, (tq, HEAD_V), 0)
        lane = lax.broadcasted_iota(jnp.int32, (tq, HEAD_V), 1)
        kfeat = jnp.where(lane < 3, lax.div(key, POS_RADIX),
                          jnp.where(lane < 6, lax.rem(key, POS_RADIX),
                                    jnp.where(lane < 9, 1, 0))).astype(BF16)
        sub = lax.broadcasted_iota(jnp.int32, (ACC_ROWS - HEAD_V, tq), 0)
        ones_row = jnp.where(sub == 0, 1.0, 0.0).astype(BF16)
        for c in range(n_chunks):
            kb_ref[c, :, :HEAD_V] = k_ref[c * tq:(c + 1) * tq, :].astype(BF16)
            kb_ref[c, :, HEAD_V:] = kfeat
            vt_ref[c, :HEAD_V, :] = v_ref[c * tq:(c + 1) * tq, :].T.astype(BF16)
            vt_ref[c, HEAD_V:, :] = ones_row
        pad = jnp.zeros((HEAD_V - n_meta, HEAD_V), F32)
        kmb_ref[:, :HEAD_V] = jnp.concatenate([km_ref[...], pad], axis=0).astype(BF16)
        kmb_ref[:, HEAD_V:] = jnp.zeros((HEAD_V, HEAD_V), BF16)
        vmt_ref[:HEAD_V, :] = jnp.concatenate([vm_ref[...], pad], axis=0).T.astype(BF16)
        vmt_ref[HEAD_V:, :] = ones_row[:, :HEAD_V]
        key = lax.broadcasted_iota(jnp.int32, (tq, 2 * tq), 0)
        qry = lax.broadcasted_iota(jnp.int32, (tq, 2 * tq), 1)
        qry = jnp.where(qry >= tq, qry - tq, qry)
        mask_ref[...] = jnp.where(key <= qry, 0.0, MASK_VALUE)
        qrow = lax.broadcasted_iota(jnp.int32, (2 * tq, HEAD_V), 0)
        lane = lax.broadcasted_iota(jnp.int32, (2 * tq, HEAD_V), 1)
        qrow = jnp.where(qrow >= tq, qrow - tq, qrow).astype(F32)
        whole = jnp.where(lane < 3, slope * POS_RADIX, jnp.where(lane < 6, slope, -slope * qrow))
        piece1 = whole.astype(BF16).astype(F32)
        rest = whole - piece1
        piece2 = rest.astype(BF16).astype(F32)
        piece3 = rest - piece2
        third = lax.rem(lane, 3)
        qfeat = jnp.where(third == 0, piece1, jnp.where(third == 1, piece2, piece3))
        q2_ref[:, HEAD_V:] = jnp.where(lane < 9, qfeat, 0.0).astype(BF16)

    q2_ref[:, :HEAD_V] = _stack_maps(q_ref[...], QK_SCALE * LOG2E)

    def scores(c, s_ref):
        s_ref[...] = lax.dot_general(kb_ref[c], q2_ref[...], _NT, preferred_element_type=F32)

    scores(0, sa_ref)

    key = lax.broadcasted_iota(jnp.int32, (HEAD_V, 2 * tq), 0)
    qry = lax.broadcasted_iota(jnp.int32, (HEAD_V, 2 * tq), 1)
    qry = jnp.where(qry >= tq, qry - tq, qry)
    qpos = n_meta + i * tq + qry
    s = lax.dot_general(kmb_ref[...], q2_ref[...], _NT, preferred_element_type=F32)
    s = jnp.where(key < n_meta, s + slope * (key - qpos).astype(F32), MASK_VALUE)
    m0 = jnp.max(s, axis=0, keepdims=True)
    m_ref[...] = m0
    acc_ref[...] = jnp.dot(vmt_ref[...], jnp.exp2(s - m0).astype(BF16), preferred_element_type=F32)

    def softmax_update(c, s_ref, diagonal):
        s = s_ref[...]
        if diagonal:
            s = s + mask_ref[...]
        sigma = slope * (tq * (c - i)).astype(F32)
        m_old = m_ref[...]
        m_new = jnp.maximum(m_old, jnp.max(s, axis=0, keepdims=True) + sigma)
        p = jnp.exp2(s - (m_new - sigma)).astype(BF16)
        alpha = jnp.exp2(m_old - m_new)
        acc_ref[...] = alpha * acc_ref[...] + jnp.dot(vt_ref[c], p, preferred_element_type=F32)
        m_ref[...] = m_new

    bufs = (sa_ref, sb_ref)

    def stretch(first, n_plain, then_diagonal):
        for u in range(n_plain):
            scores(first + u + 1, bufs[(u + 1) % 2])
            softmax_update(first + u, bufs[u % 2], False)
        if then_diagonal:
            softmax_update(first + n_plain, bufs[n_plain % 2], True)

    def body(j, carry):
        stretch(CHUNKS_PER_LOOP_STEP * j, CHUNKS_PER_LOOP_STEP, False)
        return carry

    n_full = i // CHUNKS_PER_LOOP_STEP
    lax.fori_loop(0, n_full, body, 0)
    left = i - CHUNKS_PER_LOOP_STEP * n_full
    for n_plain in range(CHUNKS_PER_LOOP_STEP):
        @pl.when(left == n_plain)
        def _(n_plain=n_plain):
            stretch(i - n_plain, n_plain, True)

    lam = _diff_lambda(lq1_ref, lk1_ref, lq2_ref, lk2_ref, lam_init)
    acc = acc_ref[:HEAD_V, :]
    l = acc_ref[HEAD_V:HEAD_V + 1, :]
    o = acc[:, :tq] / l[:, :tq] - lam * (acc[:, tq:] / l[:, tq:])
    ms = jnp.mean(o * o, axis=0, keepdims=True)
    o = o * lax.rsqrt(ms + EPS) * gcol_ref[...] * (1.0 - lam_init)
    o_ref[...] = o.T.astype(o_ref.dtype)


def _prompt_attention(z_b, z_s, meta_row0, slopes, lam_vecs, subln_g, lam_init, n_meta, tq=512):
    seq = z_b.shape[0]
    tq = _row_tile(seq, tq)
    assert tq % 128 == 0 and meta_row0 % n_meta == 0 and n_meta <= HEAD_V
    assert tq <= POS_RADIX * POS_RADIX, "key index digits must be exact in bf16"
    meta_blk = meta_row0 // n_meta
    vec = pl.BlockSpec((1, HEAD_QK), lambda h, i: (0, 0))
    kern = functools.partial(_prompt_attn_kernel, tq=tq, n_meta=n_meta, lam_init=lam_init)
    return pl.pallas_call(
        kern,
        grid=(N_HEADS, seq // tq),
        in_specs=[pl.BlockSpec(memory_space=pltpu.SMEM),
                  pl.BlockSpec((tq, HEAD_V), lambda h, i: (i, h)),
                  pl.BlockSpec((seq, HEAD_V), lambda h, i: (0, N_HEADS + h)),
                  pl.BlockSpec((seq, HEAD_V), lambda h, i: (0, 2 * N_HEADS + h)),
                  pl.BlockSpec((n_meta, HEAD_V), lambda h, i: (meta_blk, N_HEADS + h)),
                  pl.BlockSpec((n_meta, HEAD_V), lambda h, i: (meta_blk, 2 * N_HEADS + h)),
                  vec, vec, vec, vec,
                  pl.BlockSpec((HEAD_V, 1), lambda h, i: (0, 0))],
        out_specs=pl.BlockSpec((tq, HEAD_V), lambda h, i: (i, h)),
        out_shape=jax.ShapeDtypeStruct((seq, D_ATTN), BF16),
        scratch_shapes=[pltpu.VMEM((seq // tq, tq, 2 * HEAD_V), BF16),
                        pltpu.VMEM((seq // tq, ACC_ROWS, tq), BF16),
                        pltpu.VMEM((HEAD_V, 2 * HEAD_V), BF16),
                        pltpu.VMEM((ACC_ROWS, HEAD_V), BF16),
                        pltpu.VMEM((2 * tq, 2 * HEAD_V), BF16),
                        pltpu.VMEM((1, 2 * tq), F32),
                        pltpu.VMEM((ACC_ROWS, 2 * tq), F32),
                        pltpu.VMEM((tq, 2 * tq), F32),
                        pltpu.VMEM((tq, 2 * tq), F32),
                        pltpu.VMEM((tq, 2 * tq), F32)],
        compiler_params=_params(2),
        name="prompt_attn",
    )(slopes, z_b, z_b, z_b, z_s, z_s, *lam_vecs, subln_g.reshape(HEAD_V, 1))


def _meta_attn_kernel(slope_ref, q_ref, k_ref, v_ref, lq1_ref, lk1_ref, lq2_ref, lk2_ref, g_ref,
                      o_ref, *, n_meta, lam_init):
    slope = slope_ref[pl.program_id(0)]
    q2 = _stack_maps(q_ref[...])
    row = lax.broadcasted_iota(jnp.int32, (2 * n_meta, n_meta), 0)
    col = lax.broadcasted_iota(jnp.int32, (2 * n_meta, n_meta), 1)
    row = jnp.where(row >= n_meta, row - n_meta, row)
    s = lax.dot_general(q2, k_ref[...].astype(BF16), _NT, preferred_element_type=F32)
    s = jnp.where(col <= row, s + slope * (col - row).astype(F32), MASK_VALUE)
    p = jnp.exp(s - jnp.max(s, axis=-1, keepdims=True))
    l = jnp.sum(p, axis=-1, keepdims=True)
    acc = jnp.dot(p.astype(BF16), v_ref[...].astype(BF16), preferred_element_type=F32)
    lam = _diff_lambda(lq1_ref, lk1_ref, lq2_ref, lk2_ref, lam_init)
    o_ref[...] = _diff_combine(acc, l, lam, g_ref[...], lam_init)


def _meta_attention(z_s, meta_row0, slopes, lam_vecs, subln_g, lam_init, n_meta):
    meta_blk = meta_row0 // n_meta
    vec = pl.BlockSpec((1, HEAD_QK), lambda h: (0, 0))
    kern = functools.partial(_meta_attn_kernel, n_meta=n_meta, lam_init=lam_init)
    return pl.pallas_call(
        kern,
        grid=(N_HEADS,),
        in_specs=[pl.BlockSpec(memory_space=pltpu.SMEM),
                  pl.BlockSpec((n_meta, HEAD_V), lambda h: (meta_blk, h)),
                  pl.BlockSpec((n_meta, HEAD_V), lambda h: (meta_blk, N_HEADS + h)),
                  pl.BlockSpec((n_meta, HEAD_V), lambda h: (meta_blk, 2 * N_HEADS + h)),
                  vec, vec, vec, vec,
                  pl.BlockSpec((1, HEAD_V), lambda h: (0, 0))],
        out_specs=pl.BlockSpec((n_meta, HEAD_V), lambda h: (0, h)),
        out_shape=jax.ShapeDtypeStruct((n_meta, D_ATTN), F32),
        compiler_params=_params(1),
        name="meta_attn",
    )(slopes, z_s, z_s, z_s, *lam_vecs, subln_g)


def _sample_attn_parts(slope_ref, q_ref, kn_ref, vn_ref, k_pages, v_pages, lam_refs, g_ref, o_ref,
                       q2_ref, m_ref, l_ref, acc_ref, bias_ref, *, page_size, ld, past, lam_init):
    lq1_ref, lk1_ref, lq2_ref, lk2_ref = lam_refs
    tk = len(k_pages) * page_size
    n_pairs = N_HEADS // 2
    hr = 2 * ld
    pr = 2 * hr
    rows = N_HEADS * hr
    pair_heads = [(j, j + n_pairs) for j in range(n_pairs)]

    slope_col = jnp.concatenate([jnp.full((hr, 1), slope_ref[h], F32)
                                 for pair in pair_heads for h in pair], axis=0)

    def row_head_and_query(shape):
        row = lax.broadcasted_iota(jnp.int32, shape, 0)
        return lax.rem(lax.div(row, hr), 2), lax.rem(row, ld)

    def prologue(first_step, seq_start):
        @pl.when(first_step)
        def _():
            rh, qi = row_head_and_query((rows, 2 * tk))
            col = lax.broadcasted_iota(jnp.int32, (rows, 2 * tk), 1)
            rel = (lax.div(col, 2) - qi).astype(F32)
            bias_ref[...] = jnp.where(lax.rem(col, 2) == rh, slope_col * rel, MASK_VALUE)

        @pl.when(seq_start)
        def _():
            for j, pair in enumerate(pair_heads):
                q2_ref[j] = jnp.concatenate(
                    [_stack_maps(q_ref[:, h * HEAD_V:(h + 1) * HEAD_V]) for h in pair], axis=0)
            m_ref[...] = jnp.full(m_ref.shape, MASK_VALUE, F32)
            l_ref[...] = jnp.zeros(l_ref.shape, F32)
            acc_ref[...] = jnp.zeros(acc_ref.shape, F32)

    def update(s, values):
        m_old = m_ref[...]
        m_new = jnp.maximum(m_old, jnp.max(s, axis=-1, keepdims=True))
        p = jnp.exp(s - m_new)
        alpha = jnp.exp(m_old - m_new)
        l_ref[...] = alpha * l_ref[...] + jnp.sum(p, axis=-1, keepdims=True)
        p = p.astype(BF16)
        pv = [jnp.dot(p[j * pr:(j + 1) * pr], values[j], preferred_element_type=F32)
              for j in range(n_pairs)]
        acc_ref[...] = alpha * acc_ref[...] + jnp.concatenate(pv, axis=0)
        m_ref[...] = m_new

    def pair_rows(pages, j):
        return jnp.concatenate([pg[0, pl.ds(j, 2 * page_size, stride=n_pairs), :] for pg in pages],
                               axis=0).astype(BF16)

    def main(c, between=None):
        s_parts = [lax.dot_general(q2_ref[j], pair_rows(k_pages, j), _NT, preferred_element_type=F32)
                   for j in range(n_pairs)]
        if between is not None:
            between()
        values = [pair_rows(v_pages, j) for j in range(n_pairs)]
        group_offset = slope_col * (c * tk - past).astype(F32)
        update(jnp.concatenate(s_parts, axis=0) + bias_ref[...] + group_offset, values)

    def epilogue(seq_end):
        @pl.when(seq_end)
        def _():
            lam = _diff_lambda(lq1_ref, lk1_ref, lq2_ref, lk2_ref, lam_init)
            rh, qi = row_head_and_query((rows, hr))
            col = lax.broadcasted_iota(jnp.int32, (rows, hr), 1)
            kj = lax.rem(col, ld)
            visible = (lax.div(col, ld) == rh) & (kj <= qi)
            s_parts, values = [], []
            for j, pair in enumerate(pair_heads):
                kn = jnp.concatenate([kn_ref[:, h * HEAD_V:(h + 1) * HEAD_V] for h in pair], axis=0)
                vn = jnp.concatenate([vn_ref[:, h * HEAD_V:(h + 1) * HEAD_V] for h in pair], axis=0)
                s_parts.append(lax.dot_general(q2_ref[j], kn.astype(BF16), _NT,
                                               preferred_element_type=F32))
                values.append(vn.astype(BF16))
            s = jnp.concatenate(s_parts, axis=0) + slope_col * (kj - qi).astype(F32)
            update(jnp.where(visible, s, MASK_VALUE), values)
            acc = acc_ref[...]
            l = l_ref[...]
            for j, pair in enumerate(pair_heads):
                for t, h in enumerate(pair):
                    r0 = j * pr + t * hr
                    o_ref[:, h * HEAD_V:(h + 1) * HEAD_V] = _diff_combine(
                        acc[r0:r0 + hr], l[r0:r0 + hr], lam, g_ref[...], lam_init)

    return prologue, main, epilogue


def _window_sums(ext):
    out = {}
    s = ext
    w = 1
    while w < max(POOL_WINDOWS):
        s = s + pltpu.roll(s, w, 0)
        w *= 2
        out[w] = s
    return out


def _pool_project(u, sums, inv_cnt, wp_ref, scale_ref):
    cg = u.shape[1] // len(POOL_WINDOWS)
    outs = []
    for g, w in enumerate(POOL_WINDOWS):
        sl = slice(g * cg, (g + 1) * cg)
        d = sums[w][:, sl] * inv_cnt[w] - u[:, sl]
        outs.append(jnp.dot(d.astype(BF16), wp_ref[g], preferred_element_type=F32))
    return jnp.concatenate(outs, axis=1) * scale_ref[...]


def _pool_out_proj_big_kernel(x_ref, a_ref, u_ref, prev_ref, meta_ref, wp_ref, scale_ref,
                              wa_ref, wm_ref, h_ref):
    halo = jnp.where(pl.program_id(0) == 0, meta_ref[...], prev_ref[...])
    u = u_ref[...]
    sums = _window_sums(jnp.concatenate([halo, u], axis=0))
    sums = {w: s[POOL_HALO:] for w, s in sums.items()}
    inv = {w: 1.0 / w for w in POOL_WINDOWS}
    m = _pool_project(u, sums, inv, wp_ref, scale_ref).astype(BF16)
    h_ref[...] = (x_ref[...]
                  + jnp.dot(a_ref[...].astype(BF16), wa_ref[...], preferred_element_type=F32)
                  + jnp.dot(m, wm_ref[...], preferred_element_type=F32))


def _pool_out_proj_big(x, a, z_b, z_s, meta_row0, w_pool, pool_scale, w_out, tm=512):
    seq, d = x.shape
    da = a.shape[1]
    d_pool = w_pool.shape[0] * w_pool.shape[1]
    tm = _row_tile(seq, tm)
    ucol = z_b.shape[1] // d_pool - 1
    per = tm // POOL_HALO
    return pl.pallas_call(
        _pool_out_proj_big_kernel,
        grid=(seq // tm,),
        in_specs=[pl.BlockSpec((tm, d), lambda i: (i, 0)),
                  pl.BlockSpec((tm, da), lambda i: (i, 0)),
                  pl.BlockSpec((tm, d_pool), lambda i: (i, ucol)),
                  pl.BlockSpec((POOL_HALO, d_pool), lambda i: (jnp.maximum(i * per - 1, 0), ucol)),
                  pl.BlockSpec((POOL_HALO, d_pool), lambda i: (meta_row0 // POOL_HALO, ucol)),
                  pl.BlockSpec(w_pool.shape, lambda i: (0, 0, 0)),
                  pl.BlockSpec((1, d_pool), lambda i: (0, 0)),
                  pl.BlockSpec((da, d), lambda i: (0, 0)),
                  pl.BlockSpec((d_pool, d), lambda i: (da // d_pool, 0))],
        out_specs=pl.BlockSpec((tm, d), lambda i: (i, 0)),
        out_shape=jax.ShapeDtypeStruct((seq, d), F32),
        compiler_params=_params(1),
        name="pool_out_proj",
    )(x, a, z_b, z_b, z_s, w_pool, pool_scale, w_out, w_out)


def _pool_small_kernel(ext_ref, wp_ref, scale_ref, m_ref, *, bd, ld, n_meta):
    grp = POOL_HALO + ld
    ext = ext_ref[...]
    sums = _window_sums(ext)
    meta0 = bd * grp + POOL_HALO

    def new_rows(a):
        parts = [a[b * grp + POOL_HALO:(b + 1) * grp] for b in range(bd)]
        return jnp.concatenate(parts + [a[meta0:meta0 + n_meta]], axis=0)

    n = bd * ld + n_meta
    r = lax.broadcasted_iota(jnp.int32, (n, 1), 0)
    inv = {}
    for w in POOL_WINDOWS:
        cnt = jnp.where(r < bd * ld, w, jnp.minimum(w, r - bd * ld + 1))
        inv[w] = 1.0 / cnt.astype(F32)
    sums = {w: new_rows(s) for w, s in sums.items()}
    m_ref[...] = _pool_project(new_rows(ext), sums, inv, wp_ref, scale_ref).astype(m_ref.dtype)


def _pool_small(ext, w_pool, pool_scale, bd, ld, n_meta):
    n = bd * ld + n_meta
    d_pool = ext.shape[1]
    kern = functools.partial(_pool_small_kernel, bd=bd, ld=ld, n_meta=n_meta)
    return pl.pallas_call(
        kern,
        grid=(1,),
        in_specs=[pl.BlockSpec(ext.shape, lambda i: (0, 0)),
                  pl.BlockSpec(w_pool.shape, lambda i: (0, 0, 0)),
                  pl.BlockSpec((1, d_pool), lambda i: (0, 0))],
        out_specs=pl.BlockSpec((n, d_pool), lambda i: (0, 0)),
        out_shape=jax.ShapeDtypeStruct((n, d_pool), BF16),
        compiler_params=_params(1),
        name="pool_small",
    )(ext, w_pool, pool_scale)


def _out_proj_kernel(x_ref, a_ref, m_ref, wa_ref, wm_ref, h_ref):
    h_ref[...] = (x_ref[...]
                  + jnp.dot(a_ref[...].astype(BF16), wa_ref[...], preferred_element_type=F32)
                  + jnp.dot(m_ref[...], wm_ref[...], preferred_element_type=F32))


def _out_proj(x, a, m, w_out, tm=512):
    rows, d = x.shape
    da, dm = a.shape[1], m.shape[1]
    tm = _row_tile(rows, tm)
    return pl.pallas_call(
        _out_proj_kernel,
        grid=(rows // tm,),
        in_specs=[pl.BlockSpec((tm, d), lambda i: (i, 0)),
                  pl.BlockSpec((tm, da), lambda i: (i, 0)),
                  pl.BlockSpec((tm, dm), lambda i: (i, 0)),
                  pl.BlockSpec((da, d), lambda i: (0, 0)),
                  pl.BlockSpec((dm, d), lambda i: (da // dm, 0))],
        out_specs=pl.BlockSpec((tm, d), lambda i: (i, 0)),
        out_shape=jax.ShapeDtypeStruct((rows, d), F32),
        compiler_params=_params(1),
        name="out_proj",
    )(x, a, m, w_out, w_out)


def _mlp_parts(h_ref, g_ref, wu_ref, wd_ref, gf_ref, y_ref, xn_ref, acc_ref):
    j = pl.program_id(1)

    def prologue():
        @pl.when(j == 0)
        def _():
            h = h_ref[...]
            xn_ref[...] = _rmsnorm(h, g_ref[...]).astype(BF16)
            acc_ref[...] = h

    def main():
        a = jnp.maximum(jnp.dot(xn_ref[...], wu_ref[...], preferred_element_type=F32), 0.0)
        acc_ref[...] += jnp.dot((a * a).astype(BF16), wd_ref[...], preferred_element_type=F32)

    def epilogue():
        @pl.when(j == pl.num_programs(1) - 1)
        def _():
            y_ref[...] = _rmsnorm(acc_ref[...], gf_ref[...])

    return prologue, main, epilogue


def _mlp_kernel(h_ref, g_ref, wu_ref, wd_ref, gf_ref, y_ref, xn_ref):
    prologue, main, epilogue = _mlp_parts(h_ref, g_ref, wu_ref, wd_ref, gf_ref, y_ref, xn_ref, y_ref)
    prologue()
    main()
    epilogue()


def _mlp_sample_kernel(pt_ref, h_ref, g_ref, wu_ref, wd_ref, gf_ref, slope_ref, q_ref, kn_ref, vn_ref,
                       *rest, pages_per_step, steps_per_seq, n_sample_steps, page_size, ld, past,
                       lam_init):
    del pt_ref
    P = pages_per_step
    k_pages, v_pages = rest[:P], rest[P:2 * P]
    lam_refs = rest[2 * P:2 * P + 4]
    (g_sub_ref, y_ref, o_ref, xn_ref, yacc_ref,
     q2_ref, m_ref, l_ref, acc_ref, bias_ref) = rest[2 * P + 4:]
    r = pl.program_id(2)
    t = (pl.program_id(0) * pl.num_programs(1) + pl.program_id(1)) * pl.num_programs(2) + r
    active = t < n_sample_steps
    c = lax.rem(jnp.minimum(t, n_sample_steps - 1), steps_per_seq)
    mlp_prologue, mlp_main, mlp_epilogue = _mlp_parts(h_ref, g_ref, wu_ref, wd_ref, gf_ref, y_ref,
                                                      xn_ref.at[r], yacc_ref.at[r])
    att_prologue, att_main, att_epilogue = _sample_attn_parts(
        slope_ref, q_ref, kn_ref, vn_ref, k_pages, v_pages, lam_refs, g_sub_ref, o_ref,
        q2_ref, m_ref, l_ref, acc_ref, bias_ref, page_size=page_size, ld=ld, past=past,
        lam_init=lam_init)
    mlp_prologue()
    att_prologue(t == 0, active & (c == 0))
    att_main(c, mlp_main)
    mlp_epilogue()
    att_epilogue(active & (c == steps_per_seq - 1))


def _mlp_with_sample_attention(h, g, w_up, w_down, g_final, z_s, cache_k, cache_v, page_table, slopes,
                               lam_vecs, subln_g, lam_init, ld, tm=512, tf=512):
    rows, d = h.shape
    d_ff = w_up.shape[1]
    tm = _row_tile(rows, tm)
    R = MLP_ROW_TILES_PER_WEIGHT_FETCH if (rows // tm) % MLP_ROW_TILES_PER_WEIGHT_FETCH == 0 else 1
    n_i, n_j = rows // (tm * R), d_ff // tf
    bd, n_pages = page_table.shape
    n_phys, page_size = cache_k.shape[0], cache_k.shape[1]
    past = n_pages * page_size
    P = min(p for p in range(1, n_pages + 1)
            if n_pages % p == 0 and bd * (n_pages // p) <= n_i * n_j * R)
    assert P <= MAX_PAGES_PER_STEP
    spp = n_pages // P
    n_steps = bd * spp
    k2 = cache_k.reshape(n_phys, page_size * N_HEADS, HEAD_V)
    v2 = cache_v.reshape(n_phys, page_size * N_HEADS, HEAD_V)

    def seq_and_group(i, j, r):
        t = jnp.minimum((i * n_j + j) * R + r, n_steps - 1)
        return t // spp, t % spp

    def seq_spec(col):
        return pl.BlockSpec((ld, D_ATTN), lambda i, j, r, pt: (seq_and_group(i, j, r)[0], col))

    def page_spec(k):
        def index(i, j, r, pt):
            b, c = seq_and_group(i, j, r)
            return pt[b * n_pages + c * P + k], 0, 0
        return pl.BlockSpec((1, page_size * N_HEADS, HEAD_V), index)

    def h_index(i, j, r, pt):
        return jnp.where(j == 0, i * R + r, i * R + R - 1), 0

    def y_index(i, j, r, pt):
        return jnp.where(j == n_j - 1, i * R + r, i * R), 0

    vec = pl.BlockSpec((1, HEAD_QK), lambda i, j, r, pt: (0, 0))
    kern = functools.partial(_mlp_sample_kernel, pages_per_step=P, steps_per_seq=spp,
                             n_sample_steps=n_steps, page_size=page_size, ld=ld, past=past,
                             lam_init=lam_init)
    score_rows = N_HEADS * 2 * ld
    grid_spec = pltpu.PrefetchScalarGridSpec(
        num_scalar_prefetch=1,
        grid=(n_i, n_j, R),
        in_specs=[pl.BlockSpec((tm, d), h_index, pipeline_mode=pl.Buffered(1)),
                  pl.BlockSpec((1, d), lambda i, j, r, pt: (0, 0)),
                  pl.BlockSpec((d, tf), lambda i, j, r, pt: (0, j)),
                  pl.BlockSpec((tf, d), lambda i, j, r, pt: (j, 0)),
                  pl.BlockSpec((1, d), lambda i, j, r, pt: (0, 0)),
                  pl.BlockSpec(memory_space=pltpu.SMEM),
                  seq_spec(0), seq_spec(1), seq_spec(2)]
                 + [page_spec(k) for k in range(P)] + [page_spec(k) for k in range(P)]
                 + [vec, vec, vec, vec, pl.BlockSpec((1, HEAD_V), lambda i, j, r, pt: (0, 0))],
        out_specs=[pl.BlockSpec((tm, d), y_index),
                   pl.BlockSpec((ld, D_ATTN), lambda i, j, r, pt: (seq_and_group(i, j, r)[0], 0))],
        scratch_shapes=[pltpu.VMEM((R, tm, d), BF16),
                        pltpu.VMEM((R, tm, d), F32),
                        pltpu.VMEM((N_HEADS // 2, 4 * ld, HEAD_V), BF16),
                        pltpu.VMEM((score_rows, 1), F32),
                        pltpu.VMEM((score_rows, 1), F32),
                        pltpu.VMEM((score_rows, HEAD_V), F32),
                        pltpu.VMEM((score_rows, 2 * P * page_size), F32)])
    return pl.pallas_call(
        kern,
        grid_spec=grid_spec,
        out_shape=[jax.ShapeDtypeStruct((rows, d), F32),
                   jax.ShapeDtypeStruct((bd * ld, D_ATTN), F32)],
        compiler_params=_params(3),
        name="mlp_sample_attn",
    )(page_table.reshape(-1), h, g, w_up, w_down, g_final, slopes, z_s, z_s, z_s,
      *([k2] * P), *([v2] * P), *lam_vecs, subln_g)


def _mlp(h, g, w_up, w_down, g_final, tm=512, tf=1024):
    rows, d = h.shape
    d_ff = w_up.shape[1]
    tm = _row_tile(rows, tm)
    return pl.pallas_call(
        _mlp_kernel,
        grid=(rows // tm, d_ff // tf),
        in_specs=[pl.BlockSpec((tm, d), lambda i, j: (i, 0)),
                  pl.BlockSpec((1, d), lambda i, j: (0, 0)),
                  pl.BlockSpec((d, tf), lambda i, j: (0, j)),
                  pl.BlockSpec((tf, d), lambda i, j: (j, 0)),
                  pl.BlockSpec((1, d), lambda i, j: (0, 0))],
        out_specs=pl.BlockSpec((tm, d), lambda i, j: (i, 0)),
        out_shape=jax.ShapeDtypeStruct((rows, d), F32),
        scratch_shapes=[pltpu.VMEM((tm, d), BF16)],
        compiler_params=_params(2),
        name="mlp",
    )(h, g, w_up, w_down, g_final)


def kernel(x_prompt, x_sample, cache_k, cache_v, state_pool, page_table, meta_tokens, norm_mix_g,
           w_in, lambda_q1, lambda_k1, lambda_q2, lambda_k2, subln_g, w_pool, pool_scale, w_out,
           norm_mlp_g, w_up, w_down, norm_final_g):
    depth = norm_mix_g.shape[0]
    batch, seq, d_model = x_prompt.shape
    bd, ld, _ = x_sample.shape
    n_meta = meta_tokens.shape[0]
    assert depth == 1 and batch == 1, "one layer and one prompt sequence are supported"
    assert n_meta == POOL_HALO and (bd * ld) % n_meta == 0 and seq >= POOL_HALO
    layer = 0
    lam_init = 0.8 - 0.6 * math.exp(-0.3 * layer)
    slopes = 2.0 ** (-8.0 * jnp.arange(1, N_HEADS + 1, dtype=F32) / N_HEADS)
    n_small = bd * ld
    d_pool = w_pool.shape[1] * w_pool.shape[2]

    x_big = x_prompt.reshape(seq, d_model)
    x_small = jnp.concatenate([x_sample.reshape(n_small, d_model), meta_tokens.astype(F32)], axis=0)

    g_mix = norm_mix_g[layer].reshape(1, d_model)
    g_mlp = norm_mlp_g[layer].reshape(1, d_model)
    g_final = norm_final_g.reshape(1, d_model)
    g_sub = subln_g[layer].reshape(1, HEAD_V)
    scale_pool = pool_scale[layer].reshape(1, d_pool)
    lam_vecs = [v[layer].reshape(1, HEAD_QK) for v in (lambda_q1, lambda_k1, lambda_q2, lambda_k2)]
    w_in_b = w_in[layer].astype(BF16)
    w_out_b = w_out[layer].astype(BF16)
    w_up_b = w_up[layer].astype(BF16)
    w_down_b = w_down[layer].astype(BF16)
    w_pool_b = w_pool[layer].astype(BF16)

    z_b, k_b, v_b = _in_proj(x_big, g_mix, w_in_b)
    z_s, k_s, v_s = _in_proj(x_small, g_mix, w_in_b)

    a_b = _prompt_attention(z_b, z_s, n_small, slopes, lam_vecs, g_sub, lam_init, n_meta)
    a_meta = _meta_attention(z_s, n_small, slopes, lam_vecs, g_sub, lam_init, n_meta)

    u_b = z_b[:, 3 * D_ATTN:]
    u_s = z_s[:n_small, 3 * D_ATTN:].reshape(bd, ld, d_pool)
    u_meta = z_s[n_small:, 3 * D_ATTN:]
    hist = state_pool[layer].astype(F32)
    n_hist = hist.shape[1]
    zpad = jnp.zeros((bd, POOL_HALO - n_hist, d_pool), F32)
    ext = jnp.concatenate([jnp.concatenate([zpad, hist, u_s], axis=1).reshape(-1, d_pool),
                           jnp.zeros((POOL_HALO, d_pool), F32), u_meta], axis=0)
    m_s = _pool_small(ext, w_pool_b, scale_pool, bd, ld, n_meta)

    h_b = _pool_out_proj_big(x_big, a_b, z_b, z_s, n_small, w_pool_b, scale_pool, w_out_b)
    y_b, a_samp = _mlp_with_sample_attention(h_b, g_mlp, w_up_b, w_down_b, g_final, z_s, cache_k[layer],
                                             cache_v[layer], page_table, slopes, lam_vecs, g_sub,
                                             lam_init, ld)
    a_s = jnp.concatenate([a_samp, a_meta], axis=0)
    h_s = _out_proj(x_small, a_s, m_s, w_out_b)
    y_s = _mlp(h_s, g_mlp, w_up_b, w_down_b, g_final)

    t = seq + n_meta
    k_prompt = jnp.concatenate([k_s[n_small:], k_b], axis=0)
    v_prompt = jnp.concatenate([v_s[n_small:], v_b], axis=0)
    u_ext = jnp.concatenate([hist, u_s], axis=1)
    return (y_b.reshape(1, seq, d_model),
            y_s[:n_small].reshape(bd, ld, d_model),
            k_prompt.reshape(1, 1, t, N_HEADS, HEAD_V),
            v_prompt.reshape(1, 1, t, N_HEADS, HEAD_V),
            u_b[-n_hist:].reshape(1, 1, n_hist, d_pool),
            k_s[:n_small].reshape(1, bd, ld, N_HEADS, HEAD_V),
            v_s[:n_small].reshape(1, bd, ld, N_HEADS, HEAD_V),
            u_ext[:, -n_hist:].reshape(1, bd, n_hist, d_pool))
```
